```python
import jax, jax.numpy as jnp
from jax import lax
import numpy as np

D_MODEL = 1024
BATCH = 2
SEQ = 16384
DEPTH = 1
DEC_BATCH = 2
DEC_SEQ = 8192
PAST_LEN = 128

D_MIX = D_MODEL
HGRN_HEADS = 4
HGRN_DK = 128
HGRN_DV = (D_MIX // 2) // HGRN_HEADS
HGRN_KW = HGRN_HEADS * HGRN_DK
HGRN_VW = HGRN_HEADS * HGRN_DV
CONV_W = D_MIX - HGRN_VW
CONV_K = 3
CHUNK = 64
N_EXPERTS = 16
EXPERT_FF = 1024
CAPACITY_FACTOR = 2
RMS_EPS = 1e-6
IN_SIZES = (HGRN_KW, HGRN_KW, HGRN_KW, HGRN_VW, HGRN_VW, CONV_W, CONV_W, CONV_W)
IN_COLS = 3 * HGRN_KW + 2 * HGRN_VW + 3 * CONV_W

kernel_name = "hybrid_hgrn2_shortconv_ec_moe_encoder"


def rmsnorm(x, g):
    xf = x.astype(jnp.float32)
    y = xf * lax.rsqrt(jnp.mean(xf * xf, axis=-1, keepdims=True) + RMS_EPS)
    return (y * g.astype(jnp.float32)).astype(x.dtype)


def modulation(c, ada_w, ada_b):
    m = jax.nn.silu(c) @ ada_w + ada_b
    return [t[:, None, :] for t in jnp.split(m, 6, axis=-1)]


def forget_gate(z, lb):
    log_f = jnp.logaddexp(jnp.log(lb), jnp.log1p(-lb) + jax.nn.log_sigmoid(z))
    k = (1.0 - lb) * jax.nn.sigmoid(-z)
    return log_f, k


def chunk_gated_recurrence(q, k, v, log_f):
    B, T, H, DK = q.shape
    DV = v.shape[-1]
    n = T // CHUNK

    def to_chunks(a):
        return a.reshape(B, n, CHUNK, H, a.shape[-1]).transpose(1, 0, 3, 2, 4)

    causal = jnp.tril(jnp.ones((CHUNK, CHUNK), dtype=bool))[:, :, None]

    def step(S, inp):
        qi, ki, vi, gi = inp
        b = jnp.cumsum(gi, axis=-2)
        o_inter = jnp.einsum("bhck,bhkv->bhcv", qi * jnp.exp(b), S)
        diff = b[..., :, None, :] - b[..., None, :, :]
        decay = jnp.exp(jnp.where(causal, diff, -jnp.inf))
        A = jnp.einsum("bhtk,bhtsk,bhsk->bhts", qi, decay, ki)
        o_intra = jnp.einsum("bhts,bhsv->bhtv", A, vi)
        b_last = b[..., -1:, :]
        S_new = jnp.exp(b_last)[..., 0, :, None] * S + jnp.einsum(
            "bhck,bhcv->bhkv", ki * jnp.exp(b_last - b), vi)
        return S_new, o_inter + o_intra

    S0 = jnp.zeros((B, H, DK, DV), jnp.float32)
    _, o = lax.scan(step, S0, (to_chunks(q), to_chunks(k), to_chunks(v), to_chunks(log_f)))
    return o.transpose(1, 0, 3, 2, 4).reshape(B, T, H, DV)


def hgrn2_group(q, zf, zb, v, og, lb, norm_g):
    B, T, _ = q.shape
    dt = q.dtype

    def heads(a, d):
        return a.astype(jnp.float32).reshape(B, T, HGRN_HEADS, d)

    qh = jax.nn.silu(heads(q, HGRN_DK))
    vh = heads(v, HGRN_DV)
    lb = lb.reshape(2, HGRN_HEADS, HGRN_DK)
    lf_f, k_f = forget_gate(heads(zf, HGRN_DK), lb[0])
    lf_b, k_b = forget_gate(heads(zb, HGRN_DK), lb[1])
    rev = lambda a: jnp.flip(a, axis=1)
    o = chunk_gated_recurrence(qh, k_f, vh, lf_f) + rev(
        chunk_gated_recurrence(rev(qh), rev(k_b), rev(vh), rev(lf_b)))
    o = o * lax.rsqrt(jnp.mean(o * o, axis=-1, keepdims=True) + RMS_EPS) * norm_g.astype(jnp.float32)
    o = o.reshape(B, T, HGRN_VW) * jax.nn.silu(og.astype(jnp.float32))
    return o.astype(dt)


def short_conv_group(gate_b, gate_c, h, conv_w):
    u = gate_c * h
    T = u.shape[1]
    pad = CONV_K // 2
    up = jnp.pad(u, ((0, 0), (pad, pad), (0, 0)))
    y = sum(conv_w[j] * up[:, j:j + T] for j in range(CONV_K))
    return gate_b * y


def expert_choice_ffn(h, router_w, w_gate, w_up, w_down):
    B, T, D = h.shape
    n_tok = B * T
    cap = CAPACITY_FACTOR * n_tok // N_EXPERTS
    tok = h.reshape(n_tok, D)
    aff = jax.nn.softmax((tok @ router_w).astype(jnp.float32), axis=-1)
    gates, idx = lax.top_k(aff.T, cap)
    xe = tok[idx]
    hid = jax.nn.silu(jnp.einsum("ecd,edf->ecf", xe, w_gate)) * jnp.einsum("ecd,edf->ecf", xe, w_up)
    ye = jnp.einsum("ecf,efd->ecd", hid, w_down) * gates[..., None].astype(h.dtype)
    out = jnp.zeros((n_tok, D), h.dtype).at[idx.reshape(-1)].add(ye.reshape(-1, D))
    return out.reshape(B, T, D)


def encoder_layer(x, c, ada_w, ada_b, g_mix, g_ffn, w_in, lb, hg_norm_g, conv_w, w_out,
                  router_w, w_gate, w_up, w_down):
    sh1, sc1, gt1, sh2, sc2, gt2 = modulation(c, ada_w, ada_b)
    h = rmsnorm(x, g_mix) * (1.0 + sc1) + sh1
    u = h @ w_in
    split_at = [int(s) for s in np.cumsum(IN_SIZES)[:-1]]
    q, zf, zb, v, og, cb, cc, ch = jnp.split(u, split_at, axis=-1)
    y_a = hgrn2_group(q, zf, zb, v, og, lb, hg_norm_g)
    y_b = short_conv_group(cb, cc, ch, conv_w)
    x = x + gt1 * (jnp.concatenate([y_a, y_b], axis=-1) @ w_out)
    h = rmsnorm(x, g_ffn) * (1.0 + sc2) + sh2
    x = x + gt2 * expert_choice_ffn(h, router_w, w_gate, w_up, w_down)
    return x


def trunk(x, c, ada_w, ada_b, norm_mix_g, norm_ffn_g, w_in, hgrn_lb_logits, hgrn_out_norm_g,
          conv_w, w_out, router_w, w_gate, w_up, w_down, final_norm_g):
    lb_all = jnp.cumsum(jax.nn.softmax(hgrn_lb_logits.astype(jnp.float32), axis=1), axis=1)
    for l in range(DEPTH):
        x = encoder_layer(x, c, ada_w[l], ada_b[l], norm_mix_g[l], norm_ffn_g[l], w_in[l],
                          lb_all[:, l], hgrn_out_norm_g[l], conv_w[l], w_out[l],
                          router_w[l], w_gate[l], w_up[l], w_down[l])
    return rmsnorm(x, final_norm_g)


def setup_inputs(seed: int = 0) -> dict:
    key = jax.random.key(seed)
    ks = jax.random.split(key, 18)
    f32 = jnp.float32

    def nrm(k, shape, scale):
        return jax.random.normal(k, shape, f32) * scale

    return {
        "x_prompt": nrm(ks[0], (BATCH, SEQ, D_MODEL), 1.0),
        "x_sample": nrm(ks[1], (DEC_BATCH, DEC_SEQ, D_MODEL), 1.0),
        "c_prompt": nrm(ks[2], (BATCH, D_MODEL), 1.0),
        "c_sample": nrm(ks[3], (DEC_BATCH, D_MODEL), 1.0),
        "ada_w": nrm(ks[4], (DEPTH, D_MODEL, 6 * D_MODEL), 0.5 * D_MODEL ** -0.5),
        "ada_b": nrm(ks[5], (DEPTH, 6 * D_MODEL), 0.02),
        "norm_mix_g": 1.0 + nrm(ks[6], (DEPTH, D_MODEL), 0.02),
        "norm_ffn_g": 1.0 + nrm(ks[7], (DEPTH, D_MODEL), 0.02),
        "w_in": nrm(ks[8], (DEPTH, D_MODEL, IN_COLS), D_MODEL ** -0.5),
        "hgrn_lb_logits": nrm(ks[9], (2, DEPTH + 1, HGRN_KW), 0.1),
        "hgrn_out_norm_g": 1.0 + nrm(ks[10], (DEPTH, HGRN_DV), 0.02),
        "conv_w": nrm(ks[11], (DEPTH, CONV_K, CONV_W), CONV_K ** -0.5),
        "w_out": nrm(ks[12], (DEPTH, D_MIX, D_MODEL), D_MIX ** -0.5),
        "router_w": nrm(ks[13], (DEPTH, D_MODEL, N_EXPERTS), D_MODEL ** -0.5),
        "w_gate": nrm(ks[14], (DEPTH, N_EXPERTS, D_MODEL, EXPERT_FF), D_MODEL ** -0.5),
        "w_up": nrm(ks[15], (DEPTH, N_EXPERTS, D_MODEL, EXPERT_FF), D_MODEL ** -0.5),
        "w_down": nrm(ks[16], (DEPTH, N_EXPERTS, EXPERT_FF, D_MODEL), EXPERT_FF ** -0.5),
        "final_norm_g": 1.0 + nrm(ks[17], (D_MODEL,), 0.02),
    }


def reference(x_prompt, x_sample, c_prompt, c_sample, ada_w, ada_b, norm_mix_g, norm_ffn_g,
              w_in, hgrn_lb_logits, hgrn_out_norm_g, conv_w, w_out, router_w, w_gate, w_up,
              w_down, final_norm_g):
    y_prompt = trunk(x_prompt, c_prompt, ada_w, ada_b, norm_mix_g, norm_ffn_g, w_in,
                     hgrn_lb_logits, hgrn_out_norm_g, conv_w, w_out, router_w, w_gate, w_up,
                     w_down, final_norm_g)
    y_sample = trunk(x_sample, c_sample, ada_w, ada_b, norm_mix_g, norm_ffn_g, w_in,
                     hgrn_lb_logits, hgrn_out_norm_g, conv_w, w_out, router_w, w_gate, w_up,
                     w_down, final_norm_g)
    return (y_prompt, y_sample)
```

```python
import functools

import jax
import jax.numpy as jnp
from jax import lax
from jax.experimental import pallas as pl
from jax.experimental.pallas import tpu as pltpu

F32 = jnp.float32
BF16 = jnp.bfloat16
I32 = jnp.int32

D_MODEL = 1024
N_HEADS = 4
HEAD_DIM = 128
KEY_W = N_HEADS * HEAD_DIM
CONV_W = D_MODEL - KEY_W
N_EXPERTS = 16
CAPACITY_FACTOR = 2
RMS_EPS = 1e-6

LANES = 128
VMEM_LIMIT_BYTES = 56 * 1024 * 1024

CHUNK = 64
CROSS_SIZES = (64, 32)
DIAG_SIZE = 16
EXP_CLAMP = 80.0

MIX_ROWS = 256
FFN_ROWS = 256
COMB_ROWS = 256
COMB_WIN = 64


def _dot(a, b):
    return jnp.dot(a, b, preferred_element_type=F32)


def _dot_nt(a, b):
    return lax.dot_general(a, b, (((1,), (1,)), ((), ())), preferred_element_type=F32)


def _dot_tn(a, b):
    return lax.dot_general(a, b, (((0,), (0,)), ((), ())), preferred_element_type=F32)


def _split2(x):
    hi = x.astype(BF16)
    lo = (x - hi.astype(F32)).astype(BF16)
    return hi, lo


def _dot3(a, b_hi, b_lo):
    a_hi, a_lo = _split2(a)
    return _dot(a_hi, b_hi) + _dot(a_lo, b_hi) + _dot(a_hi, b_lo)


def _silu(x):
    return x / (1.0 + jnp.exp(-x))


def _modulated_norm(x, gain, scale, shift):
    ms = jnp.mean(x * x, axis=-1, keepdims=True)
    return x * lax.rsqrt(ms + RMS_EPS) * gain * (1.0 + scale) + shift


def _modulation_kernel(c_ref, w_ref, b_ref, o_ref):
    w_hi, w_lo = _split2(w_ref[...])
    o_ref[...] = _dot3(_silu(c_ref[...]), w_hi, w_lo) + b_ref[...]


def _modulation(c_rows, ada_w, ada_b):
    rows, d = c_rows.shape
    cols = ada_w.shape[1]
    bn = 1024
    return pl.pallas_call(
        _modulation_kernel,
        out_shape=jax.ShapeDtypeStruct((rows, cols), F32),
        grid=(cols // bn,),
        in_specs=[
            pl.BlockSpec((rows, d), lambda j: (0, 0)),
            pl.BlockSpec((d, bn), lambda j: (0, j)),
            pl.BlockSpec((1, bn), lambda j: (0, j)),
        ],
        out_specs=pl.BlockSpec((rows, bn), lambda j: (0, j)),
        compiler_params=pltpu.CompilerParams(
            dimension_semantics=("arbitrary",), vmem_limit_bytes=VMEM_LIMIT_BYTES),
        name="modulation",
    )(c_rows, ada_w, ada_b.reshape(1, cols))


def _lower_bound(logits):
    e = jnp.exp(logits - jnp.max(logits, axis=0, keepdims=True))
    return e[0:1] / jnp.sum(e, axis=0, keepdims=True)


def _forget_gate(z, lb):
    e = jnp.exp(-jnp.abs(z))
    big = 1.0 / (1.0 + e)
    small = e * big
    pos = z >= 0.0
    f = lb + (1.0 - lb) * jnp.where(pos, big, small)
    k = (1.0 - lb) * jnp.where(pos, small, big)
    return jnp.log(f), k


def _pair_masks(reverse):
    t = lax.broadcasted_iota(I32, (CHUNK, CHUNK), 0)
    s = lax.broadcasted_iota(I32, (CHUNK, CHUNK), 1)
    if reverse:
        t, s = s, t
    masks = []
    for d in CROSS_SIZES:
        sh = d.bit_length() - 1
        same = (t >> sh) == (s >> sh)
        masks.append(same & ((t & (d - 1)) >= d // 2) & ((s & (d - 1)) < d // 2))
    sh = DIAG_SIZE.bit_length() - 1
    masks.append(((t >> sh) == (s >> sh)) & (s <= t))
    return masks


def _cumsum_matrix(reverse):
    t = lax.broadcasted_iota(I32, (CHUNK, CHUNK), 0)
    s = lax.broadcasted_iota(I32, (CHUNK, CHUNK), 1)
    keep = (s >= t) if reverse else (s <= t)
    return jnp.where(keep, 1.0, 0.0).astype(BF16)


def _block_reference(b, d, reverse):
    h = d // 2 if reverse else d // 2 - 1
    if d == CHUNK:
        return b[h:h + 1, :]
    b3 = b.reshape(CHUNK // d, d, HEAD_DIM)
    return jnp.broadcast_to(b3[:, h:h + 1, :], b3.shape).reshape(CHUNK, HEAD_DIM)


def _chunk_head(q, k, v, b, st, masks, reverse):
    last = 0 if reverse else CHUNK - 1
    b_last = b[last:last + 1, :]
    o = _dot_nt((q * jnp.exp(b)).astype(BF16), st.astype(BF16))
    a = None
    for d, m in zip(CROSS_SIZES, masks[:-1]):
        r = _block_reference(b, d, reverse)
        p = (q * jnp.exp(jnp.minimum(b - r, 0.0))).astype(BF16)
        g = (k * jnp.exp(jnp.minimum(r - b, 0.0))).astype(BF16)
        pg = _dot_nt(p, g)
        a = jnp.where(m, pg, 0.0) if a is None else jnp.where(m, pg, a)
    r = _block_reference(b, DIAG_SIZE, reverse)
    e = jnp.clip(b - r, -EXP_CLAMP, EXP_CLAMP)
    pg = _dot_nt((q * jnp.exp(e)).astype(BF16), (k * jnp.exp(-e)).astype(BF16))
    a = jnp.where(masks[-1], pg, a)
    v16 = v.astype(BF16)
    o = o + _dot(a.astype(BF16), v16)
    k_dec = (k * jnp.exp(b_last - b)).astype(BF16)
    st_new = st * jnp.exp(b_last) + _dot_tn(v16, k_dec)
    return o, st_new


def _recurrence(q_s, k_s, v_s, lf_s, o_dst, st_ref, n_chunks, reverse):
    masks = _pair_masks(reverse)
    tri = _cumsum_matrix(reverse)

    def body(ci, carry):
        c = (n_chunks - 1 - ci) if reverse else ci
        rows = pl.ds(pl.multiple_of(c * CHUNK, CHUNK), CHUNK)
        lf_hi, lf_lo = _split2(lf_s[rows, :])
        b_all = _dot(tri, lf_hi) + _dot(tri, lf_lo)
        for hd in range(N_HEADS):
            cols = slice(hd * HEAD_DIM, (hd + 1) * HEAD_DIM)
            o, st_new = _chunk_head(q_s[rows, cols], k_s[rows, cols], v_s[rows, cols],
                                    b_all[:, cols], st_ref[hd], masks, reverse)
            st_ref[hd] = st_new
            o_dst[rows, cols] = o
        return carry

    lax.fori_loop(0, n_chunks, body, 0)


def _project_and_gate(x_ref, mod_ref, g_ref, lbl_ref, w_ref, q_s, k_s, v_s, lf_s):
    x = x_ref[0]
    mod = mod_ref[0]
    h = _modulated_norm(x, g_ref[...], mod[1:2], mod[0:1])
    u = _dot(h.astype(BF16), w_ref[...])
    q_s[...] = _silu(u[:, 0:KEY_W])
    lf, k = _forget_gate(u[:, KEY_W:2 * KEY_W], _lower_bound(lbl_ref[0]))
    lf_s[...] = lf
    k_s[...] = k
    v_s[...] = u[:, 2 * KEY_W:3 * KEY_W]
    return x, mod, u[:, 3 * KEY_W:]


def _mixer_bwd_kernel(x_ref, mod_ref, g_ref, lbl_ref, w_ref, ob_ref, uc_ref,
                      q_s, k_s, v_s, lf_s, st_ref):
    @pl.when(pl.program_id(1) == 0)
    def _():
        st_ref[...] = jnp.zeros_like(st_ref)

    _, _, rest = _project_and_gate(x_ref, mod_ref, g_ref, lbl_ref, w_ref, q_s, k_s, v_s, lf_s)
    uc_ref[0] = rest[:, 0:CONV_W] * rest[:, CONV_W:2 * CONV_W]
    _recurrence(q_s, k_s, v_s, lf_s, ob_ref.at[0], st_ref, MIX_ROWS // CHUNK, reverse=True)


def _mixer_fwd_kernel(x_ref, mod_ref, g_ref, lbl_ref, w_ref, ob_ref, uc_ref, ucp_ref, ucn_ref,
                      ng_ref, cw_ref, wo_ref, g2_ref, rwh_ref, rwl_ref,
                      x1_ref, h2_ref, aff_ref,
                      q_s, k_s, v_s, lf_s, o_s, st_ref):
    i = pl.program_id(1)
    n_i = pl.num_programs(1)

    @pl.when(i == 0)
    def _():
        st_ref[...] = jnp.zeros_like(st_ref)

    x, mod, rest = _project_and_gate(x_ref, mod_ref, g_ref, lbl_ref, w_ref, q_s, k_s, v_s, lf_s)
    og = rest[:, 0:KEY_W]
    cb = rest[:, KEY_W:KEY_W + CONV_W]
    _recurrence(q_s, k_s, v_s, lf_s, o_s, st_ref, MIX_ROWS // CHUNK, reverse=False)

    o = o_s[...] + ob_ref[0]
    ng = ng_ref[...]
    heads = []
    for hd in range(N_HEADS):
        oh = o[:, hd * HEAD_DIM:(hd + 1) * HEAD_DIM]
        ms = jnp.mean(oh * oh, axis=-1, keepdims=True)
        heads.append(oh * lax.rsqrt(ms + RMS_EPS) * ng)
    ya = jnp.concatenate(heads, axis=1) * _silu(og)

    uc = uc_ref[0]
    row = lax.broadcasted_iota(I32, (MIX_ROWS, 1), 0)
    prev_row = jnp.where(i == 0, 0.0, ucp_ref[0][7:8, :])
    next_row = jnp.where(i == n_i - 1, 0.0, ucn_ref[0][0:1, :])
    u_prev = jnp.where(row == 0, prev_row, pltpu.roll(uc, 1, 0))
    u_next = jnp.where(row == MIX_ROWS - 1, next_row, pltpu.roll(uc, MIX_ROWS - 1, 0))
    cw = cw_ref[...]
    yb = cb * (cw[0:1] * u_prev + cw[1:2] * uc + cw[2:3] * u_next)

    y = jnp.concatenate([ya, yb], axis=1).astype(BF16)
    x1 = x + mod[2:3] * _dot(y, wo_ref[...])
    x1_ref[0] = x1
    h2 = _modulated_norm(x1, g2_ref[...], mod[4:5], mod[3:4])
    h2_ref[0] = h2

    logits = _dot3(h2, rwh_ref[...], rwl_ref[...])
    lt = logits.T[0:N_EXPERTS, :]
    ex = jnp.exp(lt - jnp.max(lt, axis=0, keepdims=True))
    aff_ref[...] = ex / jnp.sum(ex, axis=0, keepdims=True)


def _mixer(x, mod, g_mix, lb_logits, w_b, w_f, norm_g, conv_w, w_out, g_ffn, rw_hi, rw_lo):
    bsz, t_len, d = x.shape
    n_i = t_len // MIX_ROWS
    r8 = MIX_ROWS // 8
    cparams = pltpu.CompilerParams(
        dimension_semantics=("arbitrary", "arbitrary"), vmem_limit_bytes=VMEM_LIMIT_BYTES)
    rec_scratch = [pltpu.VMEM((MIX_ROWS, KEY_W), F32) for _ in range(4)]
    state = pltpu.VMEM((N_HEADS, HEAD_DIM, HEAD_DIM), F32)
    row_block = lambda width, imap: pl.BlockSpec((1, MIX_ROWS, width), imap)
    const2 = lambda shape: pl.BlockSpec(shape, lambda b, i: (0, 0))
    rev = lambda b, i: (b, n_i - 1 - i, 0)
    fwd = lambda b, i: (b, i, 0)
    n_lb = lb_logits.shape[1]

    o_b, uc = pl.pallas_call(
        _mixer_bwd_kernel,
        out_shape=(jax.ShapeDtypeStruct((bsz, t_len, KEY_W), F32),
                   jax.ShapeDtypeStruct((bsz, t_len, CONV_W), F32)),
        grid=(bsz, n_i),
        in_specs=[
            row_block(d, rev),
            pl.BlockSpec((1, 6, d), lambda b, i: (b, 0, 0)),
            const2((1, d)),
            pl.BlockSpec((1, n_lb, KEY_W), lambda b, i: (1, 0, 0)),
            const2(w_b.shape),
        ],
        out_specs=(row_block(KEY_W, rev), row_block(CONV_W, rev)),
        scratch_shapes=rec_scratch + [state],
        compiler_params=cparams,
        name="mixer_bwd",
    )(x, mod, g_mix, lb_logits, w_b)

    x1, h2, aff_t = pl.pallas_call(
        _mixer_fwd_kernel,
        out_shape=(jax.ShapeDtypeStruct((bsz, t_len, d), F32),
                   jax.ShapeDtypeStruct((bsz, t_len, d), F32),
                   jax.ShapeDtypeStruct((N_EXPERTS, bsz * t_len), F32)),
        grid=(bsz, n_i),
        in_specs=[
            row_block(d, fwd),
            pl.BlockSpec((1, 6, d), lambda b, i: (b, 0, 0)),
            const2((1, d)),
            pl.BlockSpec((1, n_lb, KEY_W), lambda b, i: (0, 0, 0)),
            const2(w_f.shape),
            row_block(KEY_W, fwd),
            row_block(CONV_W, fwd),
            pl.BlockSpec((1, 8, CONV_W), lambda b, i: (b, jnp.maximum(i * r8 - 1, 0), 0)),
            pl.BlockSpec((1, 8, CONV_W), lambda b, i: (b, jnp.minimum((i + 1) * r8, n_i * r8 - 1), 0)),
            const2((1, HEAD_DIM)),
            const2(conv_w.shape),
            const2(w_out.shape),
            const2((1, d)),
            const2(rw_hi.shape),
            const2(rw_lo.shape),
        ],
        out_specs=(row_block(d, fwd), row_block(d, fwd),
                   pl.BlockSpec((N_EXPERTS, MIX_ROWS), lambda b, i: (0, b * n_i + i))),
        scratch_shapes=rec_scratch + [pltpu.VMEM((MIX_ROWS, KEY_W), F32), state],
        compiler_params=cparams,
        name="mixer_fwd",
    )(x, mod, g_mix, lb_logits, w_f, o_b, uc, uc, uc, norm_g, conv_w, w_out, g_ffn, rw_hi, rw_lo)
    return x1, h2, aff_t


def _threshold_kernel(a_ref, thr_ref, need_ref, *, cap):
    bits = pltpu.bitcast(a_ref[...], I32)

    def count(m):
        return jnp.sum(jnp.where(m, 1.0, 0.0), axis=1, keepdims=True)

    def body(it, thr):
        cand = thr | jnp.left_shift(jnp.int32(1), 30 - it)
        return jnp.where(count(bits >= cand) >= cap, cand, thr)

    thr = lax.fori_loop(0, 31, body, jnp.zeros((N_EXPERTS, 1), I32))
    need = cap - count(bits > thr)
    thr_ref[...] = jnp.broadcast_to(thr, thr_ref.shape)
    need_ref[...] = jnp.broadcast_to(need, need_ref.shape)


def _slot_index_kernel(a_ref, thr_ref, need_ref, idx_ref, pos_ref, offs_ref, *, cap, n_tiles):
    e = pl.program_id(0)
    bits = pltpu.bitcast(a_ref[0], I32)
    thr = thr_ref[0]
    need = need_ref[0]
    lane = lax.broadcasted_iota(I32, (n_tiles, LANES), 1)
    upper = jnp.where(lax.broadcasted_iota(I32, (LANES, LANES), 0)
                      <= lax.broadcasted_iota(I32, (LANES, LANES), 1), 1.0, 0.0).astype(BF16)
    below = jnp.where(lax.broadcasted_iota(I32, (n_tiles, n_tiles), 1)
                      < lax.broadcasted_iota(I32, (n_tiles, n_tiles), 0), 1.0, 0.0).astype(BF16)

    def running_count(m):
        within = _dot(jnp.where(m, 1.0, 0.0).astype(BF16), upper)
        total = jnp.broadcast_to(within[:, LANES - 1:LANES], within.shape)
        before = _dot(below, total.astype(BF16))
        return within, before, total

    gt = bits > thr
    eq = bits == thr
    w_eq, b_eq, _ = running_count(eq)
    sel = gt | (eq & ((w_eq + b_eq) <= need))
    within, before, total = running_count(sel)
    cum = within + before
    pos_ref[0] = jnp.where(sel, cum - 1.0 + (e * cap).astype(F32), -1.0).astype(I32)
    offs_ref[0] = before.astype(I32)

    slot = lax.broadcasted_iota(I32, (n_tiles, cap), 1).astype(F32)
    lo = jnp.broadcast_to(before[:, 0:1], (n_tiles, cap))
    hi = jnp.broadcast_to((before + total)[:, 0:1], (n_tiles, cap))
    in_tile = jnp.where((lo <= slot) & (slot < hi), 1.0, 0.0).astype(BF16)
    b_hi = jnp.floor(before * (1.0 / 256.0))
    b_lo = before - 256.0 * b_hi
    tile_id = lax.broadcasted_iota(I32, (n_tiles, LANES), 0).astype(F32)
    extras = jnp.where(lane == 0, b_hi, jnp.where(lane == 1, b_lo, jnp.where(lane == 2, tile_id, 0.0)))
    table = jnp.concatenate([within, extras], axis=1).T.astype(BF16)
    picked = _dot(table, in_tile)
    slot_row = slot[0:1, :]
    before_p = picked[LANES:LANES + 1, :] * 256.0 + picked[LANES + 1:LANES + 2, :]
    tile_p = picked[LANES + 2:LANES + 3, :]
    lane_p = jnp.sum(jnp.where(picked[0:LANES, :] <= slot_row - before_p, 1.0, 0.0),
                     axis=0, keepdims=True)
    idx_ref[0] = (tile_p * float(LANES) + lane_p).astype(I32)


def _route(aff_t, cap):
    n = aff_t.shape[1]
    n_tiles = n // LANES
    cparams1 = pltpu.CompilerParams(dimension_semantics=("arbitrary",),
                                    vmem_limit_bytes=VMEM_LIMIT_BYTES)
    thr, need = pl.pallas_call(
        functools.partial(_threshold_kernel, cap=cap),
        out_shape=(jax.ShapeDtypeStruct((N_EXPERTS, LANES), I32),
                   jax.ShapeDtypeStruct((N_EXPERTS, LANES), F32)),
        grid=(1,),
        in_specs=[pl.BlockSpec((N_EXPERTS, n), lambda i: (0, 0))],
        out_specs=(pl.BlockSpec((N_EXPERTS, LANES), lambda i: (0, 0)),
                   pl.BlockSpec((N_EXPERTS, LANES), lambda i: (0, 0))),
        compiler_params=cparams1,
        name="route_threshold",
    )(aff_t)
    per_expert = lambda width: pl.BlockSpec((1, 1, width), lambda e: (e, 0, 0))
    tiles = pl.BlockSpec((1, n_tiles, LANES), lambda e: (e, 0, 0))
    idx, pos, offs = pl.pallas_call(
        functools.partial(_slot_index_kernel, cap=cap, n_tiles=n_tiles),
        out_shape=(jax.ShapeDtypeStruct((N_EXPERTS, 1, cap), I32),
                   jax.ShapeDtypeStruct((N_EXPERTS, n_tiles, LANES), I32),
                   jax.ShapeDtypeStruct((N_EXPERTS, n_tiles, LANES), I32)),
        grid=(N_EXPERTS,),
        in_specs=[tiles, per_expert(LANES), per_expert(LANES)],
        out_specs=(per_expert(cap), tiles, tiles),
        compiler_params=cparams1,
        name="route_slots",
    )(aff_t.reshape(N_EXPERTS, n_tiles, LANES),
      thr.reshape(N_EXPERTS, 1, LANES), need.reshape(N_EXPERTS, 1, LANES))
    return idx, pos, offs


def _ffn_kernel(idx_hbm, h_hbm, rwh_ref, rwl_ref, wg_ref, wu_ref, wd_ref, ye_ref,
                idx_s, buf, sem_i, sem_g, *, n_steps):
    e = pl.program_id(0)
    step = e * pl.num_programs(1) + pl.program_id(1)
    slot = step % 2
    nxt = 1 - slot

    def idx_copy(s, sl):
        return pltpu.make_async_copy(idx_hbm.at[s], idx_s.at[sl], sem_i.at[sl])

    def gather_rows(sl):
        def body(r, carry):
            pltpu.make_async_copy(h_hbm.at[pl.ds(idx_s[sl, r], 1), :],
                                  buf.at[sl, pl.ds(r, 1), :], sem_g.at[sl]).start()
            return carry
        lax.fori_loop(0, FFN_ROWS, body, 0)

    @pl.when(step == 0)
    def _():
        idx_copy(0, 0).start()
        idx_copy(0, 0).wait()
        gather_rows(0)
        if n_steps > 1:
            idx_copy(1, 1).start()

    @pl.when(step + 1 < n_steps)
    def _():
        idx_copy(step + 1, nxt).wait()
        gather_rows(nxt)

    @pl.when(step + 2 < n_steps)
    def _():
        idx_copy(step + 2, slot).start()

    pltpu.make_async_copy(h_hbm.at[pl.ds(0, FFN_ROWS), :], buf.at[slot], sem_g.at[slot]).wait()

    xe = buf[slot]
    logits = _dot3(xe, rwh_ref[...], rwl_ref[...])
    lane = lax.broadcasted_iota(I32, logits.shape, 1)
    logits = jnp.where(lane < N_EXPERTS, logits, -jnp.inf)
    ex = jnp.exp(logits - jnp.max(logits, axis=-1, keepdims=True))
    gate = (jnp.sum(jnp.where(lane == e, ex, 0.0), axis=-1, keepdims=True)
            / jnp.sum(ex, axis=-1, keepdims=True))

    x16 = xe.astype(BF16)
    hid = _silu(_dot(x16, wg_ref[0])) * _dot(x16, wu_ref[0])
    ye_ref[...] = _dot(hid.astype(BF16), wd_ref[0]) * gate


def _expert_ffn(idx, h_rows, rw_hi, rw_lo, w_gate, w_up, w_down):
    cap = idx.shape[-1]
    n_j = cap // FFN_ROWS
    n_steps = N_EXPERTS * n_j
    d = h_rows.shape[1]
    ff = w_gate.shape[2]
    const2 = lambda shape: pl.BlockSpec(shape, lambda e, j: (0, 0))
    per_expert = lambda shape: pl.BlockSpec((1,) + shape, lambda e, j: (e, 0, 0))
    return pl.pallas_call(
        functools.partial(_ffn_kernel, n_steps=n_steps),
        out_shape=jax.ShapeDtypeStruct((N_EXPERTS * cap, d), F32),
        grid=(N_EXPERTS, n_j),
        in_specs=[
            pl.BlockSpec(memory_space=pl.ANY),
            pl.BlockSpec(memory_space=pl.ANY),
            const2(rw_hi.shape), const2(rw_lo.shape),
            per_expert((d, ff)), per_expert((d, ff)), per_expert((ff, d)),
        ],
        out_specs=pl.BlockSpec((FFN_ROWS, d), lambda e, j: (e * n_j + j, 0)),
        scratch_shapes=[
            pltpu.SMEM((2, FFN_ROWS), I32),
            pltpu.VMEM((2, FFN_ROWS, d), F32),
            pltpu.SemaphoreType.DMA((2,)),
            pltpu.SemaphoreType.DMA((2,)),
        ],
        compiler_params=pltpu.CompilerParams(
            dimension_semantics=("arbitrary", "arbitrary"), vmem_limit_bytes=VMEM_LIMIT_BYTES),
        name="expert_ffn",
    )(idx.reshape(n_steps, FFN_ROWS), h_rows, rw_hi, rw_lo, w_gate, w_up, w_down)


def _combine_kernel(offs_ref, x1_ref, pos_ref, mod_ref, g_ref, ye_hbm, out_ref, win, acc, sem,
                    *, cap, total_rows):
    i = pl.program_id(0)
    lane = lax.broadcasted_iota(I32, (1, LANES), 1)

    starts = []
    spans = []
    for e in range(N_EXPERTS):
        first = e * cap + offs_ref[e, i]
        start = (first // 8) * 8
        starts.append(start)
        spans.append(e * cap + offs_ref[e, i + 1] - start)
    span = functools.reduce(jnp.maximum, spans)
    n_rounds = (span + COMB_WIN - 1) // COMB_WIN

    expand = jnp.where(
        lax.broadcasted_iota(I32, (LANES, N_EXPERTS * COMB_WIN), 0)
        == lax.broadcasted_iota(I32, (LANES, N_EXPERTS * COMB_WIN), 1) // COMB_WIN, 1.0, 0.0).astype(BF16)
    in_win = (lax.broadcasted_iota(I32, (1, N_EXPERTS * COMB_WIN), 1) % COMB_WIN).astype(F32)
    pos = pos_ref[...]

    acc[...] = jnp.zeros_like(acc)

    def round_body(rd, carry):
        want = jnp.zeros((1, LANES), I32)
        base = jnp.zeros((1, LANES), I32)
        copies = []
        for e in range(N_EXPERTS):
            want0 = starts[e] + rd * COMB_WIN
            row0 = jnp.minimum(want0, total_rows - COMB_WIN)
            want = jnp.where(lane == e, want0, want)
            base = jnp.where(lane == e, row0, base)
            cp = pltpu.make_async_copy(ye_hbm.at[pl.ds(pl.multiple_of(row0, 8), COMB_WIN), :],
                                       win.at[pl.ds(e * COMB_WIN, COMB_WIN), :], sem)
            cp.start()
            copies.append(cp)
        mine = (pos >= want) & (pos < want + COMB_WIN)
        rel = jnp.where(mine, pos - base, -1).astype(F32).astype(BF16)
        one_hot = jnp.where(_dot(rel, expand) == in_win, 1.0, 0.0).astype(BF16)
        for cp in copies:
            cp.wait()
        w_hi, w_lo = _split2(win[...])
        acc[...] += _dot(one_hot, w_hi) + _dot(one_hot, w_lo)
        return carry

    lax.fori_loop(0, n_rounds, round_body, 0)

    mod = mod_ref[0]
    y = x1_ref[...] + mod[5:6] * acc[...]
    ms = jnp.mean(y * y, axis=-1, keepdims=True)
    out_ref[...] = y * lax.rsqrt(ms + RMS_EPS) * g_ref[...]


def _combine(offs_tab, x1_rows, pos_rows, mod, final_g, ye, cap, t_len):
    n, d = x1_rows.shape
    tiles_per_seq = t_len // COMB_ROWS
    total_rows = ye.shape[0]
    return pl.pallas_call(
        functools.partial(_combine_kernel, cap=cap, total_rows=total_rows),
        out_shape=jax.ShapeDtypeStruct((n, d), F32),
        grid_spec=pltpu.PrefetchScalarGridSpec(
            num_scalar_prefetch=1,
            grid=(n // COMB_ROWS,),
            in_specs=[
                pl.BlockSpec((COMB_ROWS, d), lambda i, offs: (i, 0)),
                pl.BlockSpec((COMB_ROWS, LANES), lambda i, offs: (i, 0)),
                pl.BlockSpec((1, 6, d), lambda i, offs: (i // tiles_per_seq, 0, 0)),
                pl.BlockSpec((1, d), lambda i, offs: (0, 0)),
                pl.BlockSpec(memory_space=pl.ANY),
            ],
            out_specs=pl.BlockSpec((COMB_ROWS, d), lambda i, offs: (i, 0)),
            scratch_shapes=[
                pltpu.VMEM((N_EXPERTS * COMB_WIN, d), F32),
                pltpu.VMEM((COMB_ROWS, d), F32),
                pltpu.SemaphoreType.DMA,
            ],
        ),
        compiler_params=pltpu.CompilerParams(
            dimension_semantics=("arbitrary",), vmem_limit_bytes=VMEM_LIMIT_BYTES),
        name="moe_combine",
    )(offs_tab, x1_rows, pos_rows, mod, final_g, ye)


def _trunk(x, mod, params):
    (g_mix, g_ffn, w_b, w_f, lb_logits, norm_g, conv_w, w_out, rw_hi, rw_lo,
     w_gate, w_up, w_down, final_g) = params
    bsz, t_len, d = x.shape
    n = bsz * t_len
    cap = CAPACITY_FACTOR * n // N_EXPERTS

    x1, h2, aff_t = _mixer(x, mod, g_mix, lb_logits, w_b, w_f, norm_g, conv_w, w_out, g_ffn,
                           rw_hi, rw_lo)
    idx, pos, offs = _route(aff_t, cap)
    ye = _expert_ffn(idx, h2.reshape(n, d), rw_hi, rw_lo, w_gate, w_up, w_down)

    pos_rows = jnp.pad(pos.reshape(N_EXPERTS, n).T, ((0, 0), (0, LANES - N_EXPERTS)), constant_values=-1)
    step = COMB_ROWS // LANES
    offs_tab = jnp.concatenate(
        [offs[:, ::step, 0], jnp.full((N_EXPERTS, 1), cap, I32)], axis=1)
    y = _combine(offs_tab, x1.reshape(n, d), pos_rows, mod, final_g, ye, cap, t_len)
    return y.reshape(bsz, t_len, d)


def kernel(x_prompt, x_sample, c_prompt, c_sample, ada_w, ada_b, norm_mix_g, norm_ffn_g, w_in,
           hgrn_lb_logits, hgrn_out_norm_g, conv_w, w_out, router_w, w_gate, w_up, w_down,
           final_norm_g):
    assert ada_w.shape[0] == 1, "one encoder layer"
    d = D_MODEL
    q, zf, zb, v, og, cb, cc, ch = jnp.split(
        w_in[0], [512, 1024, 1536, 2048, 2560, 3072, 3584], axis=1)
    w_b = jnp.concatenate([q, zb, v, cc, ch], axis=1).astype(BF16)
    w_f = jnp.concatenate([q, zf, v, og, cb], axis=1).astype(BF16)
    rw = jnp.pad(router_w[0], ((0, 0), (0, LANES - N_EXPERTS)))
    rw_hi = rw.astype(BF16)
    rw_lo = (rw - rw_hi.astype(F32)).astype(BF16)
    params = (norm_mix_g[0].reshape(1, d), norm_ffn_g[0].reshape(1, d), w_b, w_f, hgrn_lb_logits,
              hgrn_out_norm_g[0].reshape(1, HEAD_DIM), conv_w[0], w_out[0].astype(BF16), rw_hi, rw_lo,
              w_gate[0].astype(BF16), w_up[0].astype(BF16), w_down[0].astype(BF16),
              final_norm_g.reshape(1, d))

    nb_p, nb_s = c_prompt.shape[0], c_sample.shape[0]
    c_rows = jnp.concatenate(
        [c_prompt, c_sample, jnp.zeros((8 - nb_p - nb_s, d), F32)], axis=0)
    mod = _modulation(c_rows, ada_w[0], ada_b[0])
    mod_p = mod[:nb_p].reshape(nb_p, 6, d)
    mod_s = mod[nb_p:nb_p + nb_s].reshape(nb_s, 6, d)
    return (_trunk(x_prompt, mod_p, params), _trunk(x_sample, mod_s, params))
```

```python
import functools

import numpy as np
import jax
import jax.numpy as jnp
from jax import lax
from jax.experimental import pallas as pl
from jax.experimental.pallas import tpu as pltpu

F32 = jnp.float32
BF16 = jnp.bfloat16
I32 = jnp.int32

D_MODEL = 1024
N_HEADS = 4
HEAD_DIM = 128
KEY_W = N_HEADS * HEAD_DIM
CONV_W = D_MODEL - KEY_W
N_EXPERTS = 16
CAPACITY_FACTOR = 2
RMS_EPS = 1e-6

LANES = 128
BF16_ROWS = 16
VMEM_LIMIT_BYTES = 56 * 1024 * 1024

CHUNK = 256
CROSS_SIZES = (256, 128, 64, 32)
DIAG_SIZE = 16
EXP_CLAMP = 80.0

MIX_ROWS = 256
ROW_W = D_MODEL + LANES
FFN_ROWS = 256
COMB_ROWS = 256
COMB_WIN = 64


def _dot(a, b):
    return jnp.dot(a, b, preferred_element_type=F32)


def _dot_nt(a, b):
    return lax.dot_general(a, b, (((1,), (1,)), ((), ())), preferred_element_type=F32)


def _dot_tn(a, b):
    return lax.dot_general(a, b, (((0,), (0,)), ((), ())), preferred_element_type=F32)


def _split2(x):
    hi = x.astype(BF16)
    lo = (x - hi.astype(F32)).astype(BF16)
    return hi, lo


def _silu(x):
    return x / (1.0 + jnp.exp(-x))


def _modulated_norm(x, gain, scale, shift):
    ms = jnp.mean(x * x, axis=-1, keepdims=True)
    return x * lax.rsqrt(ms + RMS_EPS) * gain * (1.0 + scale) + shift


def _modulation_kernel(c_ref, w_ref, b_ref, o_ref):
    w_hi, w_lo = _split2(w_ref[...])
    s_hi, s_lo = _split2(_silu(c_ref[...]))
    o_ref[...] = _dot(s_hi, w_hi) + _dot(s_lo, w_hi) + _dot(s_hi, w_lo) + b_ref[...]


def _modulation(c_rows, ada_w, ada_b):
    rows, d = c_rows.shape
    cols = ada_w.shape[1]
    bn = 1024
    return pl.pallas_call(
        _modulation_kernel,
        out_shape=jax.ShapeDtypeStruct((rows, cols), F32),
        grid=(cols // bn,),
        in_specs=[
            pl.BlockSpec((rows, d), lambda j: (0, 0)),
            pl.BlockSpec((d, bn), lambda j: (0, j)),
            pl.BlockSpec((1, bn), lambda j: (0, j)),
        ],
        out_specs=pl.BlockSpec((rows, bn), lambda j: (0, j)),
        compiler_params=pltpu.CompilerParams(
            dimension_semantics=("arbitrary",), vmem_limit_bytes=VMEM_LIMIT_BYTES),
        name="modulation",
    )(c_rows, ada_w, ada_b.reshape(1, cols))


def _lower_bound(logits):
    e = jnp.exp(logits - jnp.max(logits, axis=0, keepdims=True))
    return e[0:1] / jnp.sum(e, axis=0, keepdims=True)


def _forget_gate(z, lb):
    e = jnp.exp(-jnp.abs(z))
    big = 1.0 / (1.0 + e)
    small = e * big
    pos = z >= 0.0
    f = lb + (1.0 - lb) * jnp.where(pos, big, small)
    k = (1.0 - lb) * jnp.where(pos, small, big)
    return jnp.log(f), k


def _level_table(reverse):
    t = np.arange(CHUNK)[:, None]
    s = np.arange(CHUNK)[None, :]
    if reverse:
        t, s = s, t
    lev = np.full((CHUNK, CHUNK), -1, np.int32)
    for l, d in enumerate(CROSS_SIZES):
        lev[(t // d == s // d) & (t % d >= d // 2) & (s % d < d // 2)] = l
    lev[(t // DIAG_SIZE == s // DIAG_SIZE) & (s <= t)] = len(CROSS_SIZES)
    return jnp.asarray(lev)


def _cumsum_table(reverse):
    t = np.arange(CHUNK)[:, None]
    s = np.arange(CHUNK)[None, :]
    keep = (s >= t) if reverse else (s <= t)
    return jnp.asarray(keep.astype(np.float32)).astype(BF16)


def _block_reference(b, d, reverse):
    h = d // 2 if reverse else d // 2 - 1
    if d == CHUNK:
        return b[h:h + 1, :]
    b3 = b.reshape(CHUNK // d, d, HEAD_DIM)
    return jnp.broadcast_to(b3[:, h:h + 1, :], b3.shape).reshape(CHUNK, HEAD_DIM)


def _chunk_head(q, k, v, b, st, lev_ref, reverse):
    last = 0 if reverse else CHUNK - 1
    b_last = b[last:last + 1, :]
    o = _dot_nt((q * jnp.exp(b)).astype(BF16), st.astype(BF16))
    a = jnp.zeros((CHUNK, CHUNK), F32)
    for l, d in enumerate(CROSS_SIZES):
        e = jnp.exp(-jnp.abs(b - _block_reference(b, d, reverse)))
        pg = _dot_nt((q * e).astype(BF16), (k * e).astype(BF16))
        a = jnp.where(lev_ref[...] == l, pg, a)
    e = jnp.clip(b - _block_reference(b, DIAG_SIZE, reverse), -EXP_CLAMP, EXP_CLAMP)
    pg = _dot_nt((q * jnp.exp(e)).astype(BF16), (k * jnp.exp(-e)).astype(BF16))
    a = jnp.where(lev_ref[...] == len(CROSS_SIZES), pg, a)
    v16 = v.astype(BF16)
    o = o + _dot(a.astype(BF16), v16)
    k_dec = (k * jnp.exp(b_last - b)).astype(BF16)
    st_new = st * jnp.exp(b_last) + _dot_tn(v16, k_dec)
    return o, st_new


def _recurrence(q_s, k_s, v_s, lf_s, o_dst, st_ref, tri_ref, lev_ref, reverse):
    n_chunks = q_s.shape[0] // CHUNK
    for c in (range(n_chunks - 1, -1, -1) if reverse else range(n_chunks)):
        rows = slice(c * CHUNK, (c + 1) * CHUNK)
        lf_hi, lf_lo = _split2(lf_s[rows, :])
        lf_s[rows, :] = _dot(tri_ref[...], lf_hi) + _dot(tri_ref[...], lf_lo)
        for hd in range(N_HEADS):
            cols = slice(hd * HEAD_DIM, (hd + 1) * HEAD_DIM)
            o, st_new = _chunk_head(q_s[rows, cols], k_s[rows, cols], v_s[rows, cols],
                                    lf_s[rows, cols], st_ref[hd], lev_ref, reverse)
            st_ref[hd] = st_new
            o_dst[rows, cols] = o


def _project_and_gate(x_ref, mod_ref, g_ref, lbl_ref, w_ref, q_s, k_s, v_s, lf_s):
    x = x_ref[0]
    mod = mod_ref[0]
    h = _modulated_norm(x, g_ref[...], mod[1:2], mod[0:1])
    u = _dot(h.astype(BF16), w_ref[...])
    q_s[...] = _silu(u[:, 0:KEY_W])
    lf, k = _forget_gate(u[:, KEY_W:2 * KEY_W], _lower_bound(lbl_ref[0]))
    lf_s[...] = lf
    k_s[...] = k
    v_s[...] = u[:, 2 * KEY_W:3 * KEY_W]
    return x, mod, u[:, 3 * KEY_W:]


def _mixer_bwd_kernel(x_ref, mod_ref, g_ref, lbl_ref, w_ref, tri_ref, lev_ref, ob_ref, uc_ref,
                      q_s, k_s, v_s, lf_s, st_ref):
    @pl.when(pl.program_id(1) == 0)
    def _():
        st_ref[...] = jnp.zeros_like(st_ref)

    _, _, rest = _project_and_gate(x_ref, mod_ref, g_ref, lbl_ref, w_ref, q_s, k_s, v_s, lf_s)
    uc_ref[0] = rest[:, 0:CONV_W] * rest[:, CONV_W:2 * CONV_W]
    _recurrence(q_s, k_s, v_s, lf_s, ob_ref.at[0], st_ref, tri_ref, lev_ref, reverse=True)


def _mixer_fwd_kernel(x_ref, mod_ref, g_ref, lbl_ref, w_ref, tri_ref, lev_ref,
                      ob_ref, uc_ref, ucp_ref, ucn_ref, ng_ref, cw_ref, wo_ref, g2_ref, rw_ref,
                      x1_ref, h2_ref, aff_ref,
                      q_s, k_s, v_s, lf_s, o_s, st_ref):
    i = pl.program_id(1)
    n_i = pl.num_programs(1)

    @pl.when(i == 0)
    def _():
        st_ref[...] = jnp.zeros_like(st_ref)

    x, mod, rest = _project_and_gate(x_ref, mod_ref, g_ref, lbl_ref, w_ref, q_s, k_s, v_s, lf_s)
    og = rest[:, 0:KEY_W]
    cb = rest[:, KEY_W:KEY_W + CONV_W]
    _recurrence(q_s, k_s, v_s, lf_s, o_s, st_ref, tri_ref, lev_ref, reverse=False)

    o = o_s[...] + ob_ref[0]
    ng = ng_ref[...]
    heads = []
    for hd in range(N_HEADS):
        oh = o[:, hd * HEAD_DIM:(hd + 1) * HEAD_DIM]
        ms = jnp.mean(oh * oh, axis=-1, keepdims=True)
        heads.append(oh * lax.rsqrt(ms + RMS_EPS) * ng)
    ya = jnp.concatenate(heads, axis=1) * _silu(og)

    uc = uc_ref[0]
    row = lax.broadcasted_iota(I32, (MIX_ROWS, 1), 0)
    prev_row = jnp.where(i == 0, 0.0, ucp_ref[0][7:8, :])
    next_row = jnp.where(i == n_i - 1, 0.0, ucn_ref[0][0:1, :])
    u_prev = jnp.where(row == 0, prev_row, pltpu.roll(uc, 1, 0))
    u_next = jnp.where(row == MIX_ROWS - 1, next_row, pltpu.roll(uc, MIX_ROWS - 1, 0))
    cw = cw_ref[...]
    yb = cb * (cw[0:1] * u_prev + cw[1:2] * uc + cw[2:3] * u_next)

    y = jnp.concatenate([ya, yb], axis=1).astype(BF16)
    x1 = x + mod[2:3] * _dot(y, wo_ref[...])
    x1_ref[0] = x1
    h2 = _modulated_norm(x1, g2_ref[...], mod[4:5], mod[3:4])

    h_hi, h_lo = _split2(h2)
    two = _dot(h_hi, rw_ref[...])
    logits = two[:, 0:LANES] + two[:, LANES:2 * LANES] + _dot(h_lo, rw_ref[:, 0:LANES])
    lane = lax.broadcasted_iota(I32, logits.shape, 1)
    logits = jnp.where(lane < N_EXPERTS, logits, -jnp.inf)
    ex = jnp.exp(logits - jnp.max(logits, axis=-1, keepdims=True))
    aff = ex / jnp.sum(ex, axis=-1, keepdims=True)
    h2_ref[0, :, 0:D_MODEL] = h2
    h2_ref[0, :, D_MODEL:ROW_W] = aff
    aff_ref[...] = aff.T[0:N_EXPERTS, :]


def _mixer(x, mod, g_mix, lb_logits, w_b, w_f, norm_g, conv_w, w_out, g_ffn, rw_cat):
    bsz, t_len, d = x.shape
    n_i = t_len // MIX_ROWS
    r8 = MIX_ROWS // 8
    cparams = pltpu.CompilerParams(
        dimension_semantics=("arbitrary", "arbitrary"), vmem_limit_bytes=VMEM_LIMIT_BYTES)
    rec_scratch = [pltpu.VMEM((MIX_ROWS, KEY_W), F32) for _ in range(4)]
    state = pltpu.VMEM((N_HEADS, HEAD_DIM, HEAD_DIM), F32)
    row_block = lambda width, imap: pl.BlockSpec((1, MIX_ROWS, width), imap)
    const2 = lambda shape: pl.BlockSpec(shape, lambda b, i: (0, 0))
    rev = lambda b, i: (b, n_i - 1 - i, 0)
    fwd = lambda b, i: (b, i, 0)
    n_lb = lb_logits.shape[1]
    sq = (CHUNK, CHUNK)

    o_b, uc = pl.pallas_call(
        _mixer_bwd_kernel,
        out_shape=(jax.ShapeDtypeStruct((bsz, t_len, KEY_W), F32),
                   jax.ShapeDtypeStruct((bsz, t_len, CONV_W), F32)),
        grid=(bsz, n_i),
        in_specs=[
            row_block(d, rev),
            pl.BlockSpec((1, 6, d), lambda b, i: (b, 0, 0)),
            const2((1, d)),
            pl.BlockSpec((1, n_lb, KEY_W), lambda b, i: (1, 0, 0)),
            const2(w_b.shape),
            const2(sq), const2(sq),
        ],
        out_specs=(row_block(KEY_W, rev), row_block(CONV_W, rev)),
        scratch_shapes=rec_scratch + [state],
        compiler_params=cparams,
        name="mixer_bwd",
    )(x, mod, g_mix, lb_logits, w_b, _cumsum_table(True), _level_table(True))

    x1, h2, aff_t = pl.pallas_call(
        _mixer_fwd_kernel,
        out_shape=(jax.ShapeDtypeStruct((bsz, t_len, d), F32),
                   jax.ShapeDtypeStruct((bsz, t_len, ROW_W), F32),
                   jax.ShapeDtypeStruct((N_EXPERTS, bsz * t_len), F32)),
        grid=(bsz, n_i),
        in_specs=[
            row_block(d, fwd),
            pl.BlockSpec((1, 6, d), lambda b, i: (b, 0, 0)),
            const2((1, d)),
            pl.BlockSpec((1, n_lb, KEY_W), lambda b, i: (0, 0, 0)),
            const2(w_f.shape),
            const2(sq), const2(sq),
            row_block(KEY_W, fwd),
            row_block(CONV_W, fwd),
            pl.BlockSpec((1, 8, CONV_W), lambda b, i: (b, jnp.maximum(i * r8 - 1, 0), 0)),
            pl.BlockSpec((1, 8, CONV_W), lambda b, i: (b, jnp.minimum((i + 1) * r8, n_i * r8 - 1), 0)),
            const2((1, HEAD_DIM)),
            const2(conv_w.shape),
            const2(w_out.shape),
            const2((1, d)),
            const2(rw_cat.shape),
        ],
        out_specs=(row_block(d, fwd), row_block(ROW_W, fwd),
                   pl.BlockSpec((N_EXPERTS, MIX_ROWS), lambda b, i: (0, b * n_i + i))),
        scratch_shapes=rec_scratch + [pltpu.VMEM((MIX_ROWS, KEY_W), F32), state],
        compiler_params=cparams,
        name="mixer_fwd",
    )(x, mod, g_mix, lb_logits, w_f, _cumsum_table(False), _level_table(False),
      o_b, uc, uc, uc, norm_g, conv_w, w_out, g_ffn, rw_cat)
    return x1, h2, aff_t


def _threshold_kernel(a_ref, thr_ref, need_ref, *, cap):
    bits = pltpu.bitcast(a_ref[...], I32)

    def count(m):
        return jnp.sum(jnp.where(m, 1.0, 0.0), axis=1, keepdims=True)

    def body(it, thr):
        cand = thr | jnp.left_shift(jnp.int32(1), 30 - it)
        return jnp.where(count(bits >= cand) >= cap, cand, thr)

    thr = lax.fori_loop(0, 31, body, jnp.zeros((N_EXPERTS, 1), I32))
    need = cap - count(bits > thr)
    thr_ref[...] = jnp.broadcast_to(thr, thr_ref.shape)
    need_ref[...] = jnp.broadcast_to(need, need_ref.shape)


def _slot_index_kernel(a_ref, thr_ref, need_ref, idx_ref, pos_ref, offs_ref, *, cap, n_tiles):
    e = pl.program_id(0)
    bits = pltpu.bitcast(a_ref[0], I32)
    thr = thr_ref[0]
    need = need_ref[0]
    lane = lax.broadcasted_iota(I32, (n_tiles, LANES), 1)
    upper = jnp.where(lax.broadcasted_iota(I32, (LANES, LANES), 0)
                      <= lax.broadcasted_iota(I32, (LANES, LANES), 1), 1.0, 0.0).astype(BF16)
    below = jnp.where(lax.broadcasted_iota(I32, (n_tiles, n_tiles), 1)
                      < lax.broadcasted_iota(I32, (n_tiles, n_tiles), 0), 1.0, 0.0).astype(BF16)

    def running_count(m):
        within = _dot(jnp.where(m, 1.0, 0.0).astype(BF16), upper)
        total = jnp.broadcast_to(within[:, LANES - 1:LANES], within.shape)
        before = _dot(below, total.astype(BF16))
        return within, before, total

    gt = bits > thr
    eq = bits == thr
    w_eq, b_eq, _ = running_count(eq)
    sel = gt | (eq & ((w_eq + b_eq) <= need))
    within, before, total = running_count(sel)
    cum = within + before
    pos_ref[0] = jnp.where(sel, cum - 1.0 + (e * cap).astype(F32), -1.0).astype(I32)
    offs_ref[0] = before.astype(I32)

    slot = lax.broadcasted_iota(I32, (n_tiles, cap), 1).astype(F32)
    lo = jnp.broadcast_to(before[:, 0:1], (n_tiles, cap))
    hi = jnp.broadcast_to((before + total)[:, 0:1], (n_tiles, cap))
    in_tile = jnp.where((lo <= slot) & (slot < hi), 1.0, 0.0).astype(BF16)
    b_hi = jnp.floor(before * (1.0 / 256.0))
    b_lo = before - 256.0 * b_hi
    tile_id = lax.broadcasted_iota(I32, (n_tiles, LANES), 0).astype(F32)
    extras = jnp.where(lane == 0, b_hi, jnp.where(lane == 1, b_lo, jnp.where(lane == 2, tile_id, 0.0)))
    table = jnp.concatenate([within, extras], axis=1).T.astype(BF16)
    picked = _dot(table, in_tile)
    slot_row = slot[0:1, :]
    before_p = picked[LANES:LANES + 1, :] * 256.0 + picked[LANES + 1:LANES + 2, :]
    tile_p = picked[LANES + 2:LANES + 3, :]
    lane_p = jnp.sum(jnp.where(picked[0:LANES, :] <= slot_row - before_p, 1.0, 0.0),
                     axis=0, keepdims=True)
    idx_ref[0] = (tile_p * float(LANES) + lane_p).astype(I32)


def _route(aff_t, cap):
    n = aff_t.shape[1]
    n_tiles = n // LANES
    cparams1 = pltpu.CompilerParams(dimension_semantics=("arbitrary",),
                                    vmem_limit_bytes=VMEM_LIMIT_BYTES)
    thr, need = pl.pallas_call(
        functools.partial(_threshold_kernel, cap=cap),
        out_shape=(jax.ShapeDtypeStruct((N_EXPERTS, LANES), I32),
                   jax.ShapeDtypeStruct((N_EXPERTS, LANES), F32)),
        grid=(1,),
        in_specs=[pl.BlockSpec((N_EXPERTS, n), lambda i: (0, 0))],
        out_specs=(pl.BlockSpec((N_EXPERTS, LANES), lambda i: (0, 0)),
                   pl.BlockSpec((N_EXPERTS, LANES), lambda i: (0, 0))),
        compiler_params=cparams1,
        name="route_threshold",
    )(aff_t)
    per_expert = lambda width: pl.BlockSpec((1, 1, width), lambda e: (e, 0, 0))
    tiles = pl.BlockSpec((1, n_tiles, LANES), lambda e: (e, 0, 0))
    idx, pos, offs = pl.pallas_call(
        functools.partial(_slot_index_kernel, cap=cap, n_tiles=n_tiles),
        out_shape=(jax.ShapeDtypeStruct((N_EXPERTS, 1, cap), I32),
                   jax.ShapeDtypeStruct((N_EXPERTS, n_tiles, LANES), I32),
                   jax.ShapeDtypeStruct((N_EXPERTS, n_tiles, LANES), I32)),
        grid=(N_EXPERTS,),
        in_specs=[tiles, per_expert(LANES), per_expert(LANES)],
        out_specs=(per_expert(cap), tiles, tiles),
        compiler_params=cparams1,
        name="route_slots",
    )(aff_t.reshape(N_EXPERTS, n_tiles, LANES),
      thr.reshape(N_EXPERTS, 1, LANES), need.reshape(N_EXPERTS, 1, LANES))
    return idx, pos, offs


def _ffn_kernel(idx_hbm, h_hbm, wg_ref, wu_ref, wd_ref, yh_ref, yl_ref,
                idx_s, buf, sem_i, sem_g, *, n_steps):
    e = pl.program_id(0)
    step = e * pl.num_programs(1) + pl.program_id(1)
    slot = step % 2
    nxt = 1 - slot
    last = n_steps - 1

    def idx_copy(s, sl):
        return pltpu.make_async_copy(idx_hbm.at[jnp.minimum(s, last)], idx_s.at[sl], sem_i.at[sl])

    def row_copy(sl, r):
        return pltpu.make_async_copy(h_hbm.at[pl.ds(idx_s[sl, r], 1), :],
                                     buf.at[sl, pl.ds(r, 1), :], sem_g.at[sl])

    def all_rows(sl):
        return pltpu.make_async_copy(h_hbm.at[pl.ds(0, FFN_ROWS), :], buf.at[sl], sem_g.at[sl])

    @pl.when(step == 0)
    def _():
        idx_copy(0, 0).start()
        idx_copy(0, 0).wait()

        def body(r, carry):
            row_copy(0, r).start()
            return carry
        lax.fori_loop(0, FFN_ROWS, body, 0)
        idx_copy(1, 1).start()

    idx_copy(step + 1, nxt).wait()
    for r in range(FFN_ROWS):
        row_copy(nxt, r).start()
    idx_copy(step + 2, slot).start()
    all_rows(slot).wait()

    rows = buf[slot]
    lane = lax.broadcasted_iota(I32, (FFN_ROWS, LANES), 1)
    gate = jnp.sum(jnp.where(lane == e, rows[:, D_MODEL:ROW_W], 0.0), axis=-1, keepdims=True)
    x16 = rows[:, 0:D_MODEL].astype(BF16)
    hid = _silu(_dot(x16, wg_ref[0])) * _dot(x16, wu_ref[0])
    ye = _dot(hid.astype(BF16), wd_ref[0]) * gate
    y_hi, y_lo = _split2(ye)
    yh_ref[...] = y_hi
    yl_ref[...] = y_lo

    @pl.when(step == last)
    def _():
        idx_copy(step + 2, slot).wait()
        all_rows(nxt).wait()


def _expert_ffn(idx, h_rows, w_gate, w_up, w_down):
    cap = idx.shape[-1]
    n_j = cap // FFN_ROWS
    n_steps = N_EXPERTS * n_j
    d, ff = w_gate.shape[1], w_gate.shape[2]
    per_expert = lambda shape: pl.BlockSpec((1,) + shape, lambda e, j: (e, 0, 0))
    out_rows = pl.BlockSpec((FFN_ROWS, d), lambda e, j: (e * n_j + j, 0))
    out_shape = jax.ShapeDtypeStruct((N_EXPERTS * cap, d), BF16)
    return pl.pallas_call(
        functools.partial(_ffn_kernel, n_steps=n_steps),
        out_shape=(out_shape, out_shape),
        grid=(N_EXPERTS, n_j),
        in_specs=[
            pl.BlockSpec(memory_space=pl.ANY),
            pl.BlockSpec(memory_space=pl.ANY),
            per_expert((d, ff)), per_expert((d, ff)), per_expert((ff, d)),
        ],
        out_specs=(out_rows, out_rows),
        scratch_shapes=[
            pltpu.SMEM((2, FFN_ROWS), I32),
            pltpu.VMEM((2, FFN_ROWS, ROW_W), F32),
            pltpu.SemaphoreType.DMA((2,)),
            pltpu.SemaphoreType.DMA((2,)),
        ],
        compiler_params=pltpu.CompilerParams(
            dimension_semantics=("arbitrary", "arbitrary"), vmem_limit_bytes=VMEM_LIMIT_BYTES),
        name="expert_ffn",
    )(idx.reshape(n_steps, FFN_ROWS), h_rows, w_gate, w_up, w_down)


def _combine_kernel(offs_ref, x1_ref, pos_ref, mod_ref, g_ref, yh_hbm, yl_hbm, out_ref,
                    win_h, win_l, acc, sem, *, cap, total_rows):
    i = pl.program_id(0)
    slot = i % 2
    lane = lax.broadcasted_iota(I32, (1, LANES), 1)

    def first_rows(tile):
        return [((e * cap + offs_ref[e, tile]) // BF16_ROWS) * BF16_ROWS for e in range(N_EXPERTS)]

    def window_copies(want_rows, sl):
        copies = []
        for e in range(N_EXPERTS):
            row0 = pl.multiple_of(jnp.minimum(want_rows[e], total_rows - COMB_WIN), BF16_ROWS)
            dst = pl.ds(e * COMB_WIN, COMB_WIN)
            copies.append(pltpu.make_async_copy(yh_hbm.at[pl.ds(row0, COMB_WIN), :], win_h.at[sl, dst, :], sem.at[sl]))
            copies.append(pltpu.make_async_copy(yl_hbm.at[pl.ds(row0, COMB_WIN), :], win_l.at[sl, dst, :], sem.at[sl]))
        return copies

    def lane_vector(values):
        vec = jnp.zeros((1, LANES), I32)
        for e, val in enumerate(values):
            vec = jnp.where(lane == e, val, vec)
        return vec

    starts = first_rows(i)

    @pl.when(i == 0)
    def _():
        for cp in window_copies(starts, 0):
            cp.start()

    @pl.when(i + 1 < pl.num_programs(0))
    def _():
        for cp in window_copies(first_rows(i + 1), 1 - slot):
            cp.start()

    span = functools.reduce(
        jnp.maximum, [e * cap + offs_ref[e, i + 1] - starts[e] for e in range(N_EXPERTS)])
    n_rounds = (span + COMB_WIN - 1) // COMB_WIN

    expand = jnp.where(
        lax.broadcasted_iota(I32, (LANES, N_EXPERTS * COMB_WIN), 0)
        == lax.broadcasted_iota(I32, (LANES, N_EXPERTS * COMB_WIN), 1) // COMB_WIN, 1.0, 0.0).astype(BF16)
    in_win = (lax.broadcasted_iota(I32, (1, N_EXPERTS * COMB_WIN), 1) % COMB_WIN).astype(F32)
    pos = pos_ref[...]

    def spread(rd):
        want_rows = [s + rd * COMB_WIN for s in starts]
        want = lane_vector(want_rows)
        base = jnp.minimum(want, total_rows - COMB_WIN)
        mine = (pos >= want) & (pos < want + COMB_WIN)
        rel = jnp.where(mine, pos - base, -1).astype(F32).astype(BF16)
        one_hot = jnp.where(_dot(rel, expand) == in_win, 1.0, 0.0).astype(BF16)
        return want_rows, one_hot

    _, one_hot = spread(0)
    for cp in window_copies(starts, slot):
        cp.wait()
    acc[...] = _dot(one_hot, win_h[slot]) + _dot(one_hot, win_l[slot])

    def round_body(rd, carry):
        want_rows, one_hot = spread(rd)
        copies = window_copies(want_rows, slot)
        for cp in copies:
            cp.start()
        for cp in copies:
            cp.wait()
        acc[...] += _dot(one_hot, win_h[slot]) + _dot(one_hot, win_l[slot])
        return carry

    lax.fori_loop(1, n_rounds, round_body, 0)

    mod = mod_ref[0]
    y = x1_ref[...] + mod[5:6] * acc[...]
    ms = jnp.mean(y * y, axis=-1, keepdims=True)
    out_ref[...] = y * lax.rsqrt(ms + RMS_EPS) * g_ref[...]


def _combine(offs_tab, x1_rows, pos_rows, mod, final_g, y_hi, y_lo, cap, t_len):
    n, d = x1_rows.shape
    tiles_per_seq = t_len // COMB_ROWS
    total_rows = y_hi.shape[0]
    window = pltpu.VMEM((2, N_EXPERTS * COMB_WIN, d), BF16)
    return pl.pallas_call(
        functools.partial(_combine_kernel, cap=cap, total_rows=total_rows),
        out_shape=jax.ShapeDtypeStruct((n, d), F32),
        grid_spec=pltpu.PrefetchScalarGridSpec(
            num_scalar_prefetch=1,
            grid=(n // COMB_ROWS,),
            in_specs=[
                pl.BlockSpec((COMB_ROWS, d), lambda i, offs: (i, 0)),
                pl.BlockSpec((COMB_ROWS, LANES), lambda i, offs: (i, 0)),
                pl.BlockSpec((1, 6, d), lambda i, offs: (i // tiles_per_seq, 0, 0)),
                pl.BlockSpec((1, d), lambda i, offs: (0, 0)),
                pl.BlockSpec(memory_space=pl.ANY),
                pl.BlockSpec(memory_space=pl.ANY),
            ],
            out_specs=pl.BlockSpec((COMB_ROWS, d), lambda i, offs: (i, 0)),
            scratch_shapes=[window, window, pltpu.VMEM((COMB_ROWS, d), F32),
                            pltpu.SemaphoreType.DMA((2,))],
        ),
        compiler_params=pltpu.CompilerParams(
            dimension_semantics=("arbitrary",), vmem_limit_bytes=VMEM_LIMIT_BYTES),
        name="moe_combine",
    )(offs_tab, x1_rows, pos_rows, mod, final_g, y_hi, y_lo)


def _trunk(x, mod, params):
    (g_mix, g_ffn, w_b, w_f, lb_logits, norm_g, conv_w, w_out, rw_cat,
     w_gate, w_up, w_down, final_g) = params
    bsz, t_len, d = x.shape
    n = bsz * t_len
    cap = CAPACITY_FACTOR * n // N_EXPERTS

    x1, h2, aff_t = _mixer(x, mod, g_mix, lb_logits, w_b, w_f, norm_g, conv_w, w_out, g_ffn, rw_cat)
    idx, pos, offs = _route(aff_t, cap)
    y_hi, y_lo = _expert_ffn(idx, h2.reshape(n, ROW_W), w_gate, w_up, w_down)

    pos_rows = jnp.pad(pos.reshape(N_EXPERTS, n).T, ((0, 0), (0, LANES - N_EXPERTS)), constant_values=-1)
    step = COMB_ROWS // LANES
    offs_tab = jnp.concatenate(
        [offs[:, ::step, 0], jnp.full((N_EXPERTS, 1), cap, I32)], axis=1)
    y = _combine(offs_tab, x1.reshape(n, d), pos_rows, mod, final_g, y_hi, y_lo, cap, t_len)
    return y.reshape(bsz, t_len, d)


def kernel(x_prompt, x_sample, c_prompt, c_sample, ada_w, ada_b, norm_mix_g, norm_ffn_g, w_in,
           hgrn_lb_logits, hgrn_out_norm_g, conv_w, w_out, router_w, w_gate, w_up, w_down,
           final_norm_g):
    assert ada_w.shape[0] == 1, "one encoder layer"
    d = D_MODEL
    q, zf, zb, v, og, cb, cc, ch = jnp.split(
        w_in[0], [512, 1024, 1536, 2048, 2560, 3072, 3584], axis=1)
    w_b = jnp.concatenate([q, zb, v, cc, ch], axis=1).astype(BF16)
    w_f = jnp.concatenate([q, zf, v, og, cb], axis=1).astype(BF16)
    rw = jnp.pad(router_w[0], ((0, 0), (0, LANES - N_EXPERTS)))
    rw_hi = rw.astype(BF16)
    rw_cat = jnp.concatenate([rw_hi, (rw - rw_hi.astype(F32)).astype(BF16)], axis=1)
    params = (norm_mix_g[0].reshape(1, d), norm_ffn_g[0].reshape(1, d), w_b, w_f, hgrn_lb_logits,
              hgrn_out_norm_g[0].reshape(1, HEAD_DIM), conv_w[0], w_out[0].astype(BF16), rw_cat,
              w_gate[0].astype(BF16), w_up[0].astype(BF16), w_down[0].astype(BF16),
              final_norm_g.reshape(1, d))

    nb_p, nb_s = c_prompt.shape[0], c_sample.shape[0]
    c_rows = jnp.concatenate(
        [c_prompt, c_sample, jnp.zeros((8 - nb_p - nb_s, d), F32)], axis=0)
    mod = _modulation(c_rows, ada_w[0], ada_b[0])
    mod_p = mod[:nb_p].reshape(nb_p, 6, d)
    mod_s = mod[nb_p:nb_p + nb_s].reshape(nb_s, 6, d)
    return (_trunk(x_prompt, mod_p, params), _trunk(x_sample, mod_s, params))
```

```python
import functools

import numpy as np
import jax
import jax.numpy as jnp
from jax import lax
from jax.experimental import pallas as pl
from jax.experimental.pallas import tpu as pltpu

F32 = jnp.float32
BF16 = jnp.bfloat16
I32 = jnp.int32

D_MODEL = 1024
N_HEADS = 4
HEAD_DIM = 128
KEY_W = N_HEADS * HEAD_DIM
CONV_W = D_MODEL - KEY_W
N_EXPERTS = 16
CAPACITY_FACTOR = 2
RMS_EPS = 1e-6

LANES = 128
BF16_ROWS = 16
VMEM_LIMIT_BYTES = 56 * 1024 * 1024

CHUNK = 256
CROSS_SIZES = (256, 128, 64, 32)
DIAG_SIZE = 16
EXP_CLAMP = 80.0

MIX_ROWS = 256
ROW_TILES = D_MODEL // LANES + 1
FFN_ROWS = 256
COMB_ROWS = 256
COMB_WIN = 64


def _dot(a, b):
    return jnp.dot(a, b, preferred_element_type=F32)


def _dot_nt(a, b):
    return lax.dot_general(a, b, (((1,), (1,)), ((), ())), preferred_element_type=F32)


def _dot_tn(a, b):
    return lax.dot_general(a, b, (((0,), (0,)), ((), ())), preferred_element_type=F32)


def _split2(x):
    hi = x.astype(BF16)
    lo = (x - hi.astype(F32)).astype(BF16)
    return hi, lo


def _silu(x):
    return x / (1.0 + jnp.exp(-x))


def _modulated_norm(x, gain, scale, shift):
    ms = jnp.mean(x * x, axis=-1, keepdims=True)
    return x * lax.rsqrt(ms + RMS_EPS) * gain * (1.0 + scale) + shift


def _modulation_kernel(c_ref, w_ref, b_ref, o_ref):
    w_hi, w_lo = _split2(w_ref[...])
    s_hi, s_lo = _split2(_silu(c_ref[...]))
    o_ref[...] = _dot(s_hi, w_hi) + _dot(s_lo, w_hi) + _dot(s_hi, w_lo) + b_ref[...]


def _modulation(c_rows, ada_w, ada_b):
    rows, d = c_rows.shape
    cols = ada_w.shape[1]
    bn = 1024
    return pl.pallas_call(
        _modulation_kernel,
        out_shape=jax.ShapeDtypeStruct((rows, cols), F32),
        grid=(cols // bn,),
        in_specs=[
            pl.BlockSpec((rows, d), lambda j: (0, 0)),
            pl.BlockSpec((d, bn), lambda j: (0, j)),
            pl.BlockSpec((1, bn), lambda j: (0, j)),
        ],
        out_specs=pl.BlockSpec((rows, bn), lambda j: (0, j)),
        compiler_params=pltpu.CompilerParams(
            dimension_semantics=("arbitrary",), vmem_limit_bytes=VMEM_LIMIT_BYTES),
        name="modulation",
    )(c_rows, ada_w, ada_b.reshape(1, cols))


def _lower_bound(logits):
    e = jnp.exp(logits - jnp.max(logits, axis=0, keepdims=True))
    return e[0:1] / jnp.sum(e, axis=0, keepdims=True)


def _forget_gate(z, lb):
    e = jnp.exp(-jnp.abs(z))
    big = 1.0 / (1.0 + e)
    small = e * big
    pos = z >= 0.0
    f = lb + (1.0 - lb) * jnp.where(pos, big, small)
    k = (1.0 - lb) * jnp.where(pos, small, big)
    return jnp.log(f), k


def _level_table(reverse):
    t = np.arange(CHUNK)[:, None]
    s = np.arange(CHUNK)[None, :]
    if reverse:
        t, s = s, t
    lev = np.full((CHUNK, CHUNK), -1, np.int32)
    for l, d in enumerate(CROSS_SIZES):
        lev[(t // d == s // d) & (t % d >= d // 2) & (s % d < d // 2)] = l
    lev[(t // DIAG_SIZE == s // DIAG_SIZE) & (s <= t)] = len(CROSS_SIZES)
    return jnp.asarray(lev)


def _cumsum_table(reverse):
    t = np.arange(CHUNK)[:, None]
    s = np.arange(CHUNK)[None, :]
    keep = (s >= t) if reverse else (s <= t)
    return jnp.asarray(keep.astype(np.float32)).astype(BF16)


def _block_reference(b, d, reverse):
    h = d // 2 if reverse else d // 2 - 1
    if d == CHUNK:
        return b[h:h + 1, :]
    b3 = b.reshape(CHUNK // d, d, HEAD_DIM)
    return jnp.broadcast_to(b3[:, h:h + 1, :], b3.shape).reshape(CHUNK, HEAD_DIM)


def _chunk_head(q, k, v, b, st, lev_ref, reverse):
    last = 0 if reverse else CHUNK - 1
    b_last = b[last:last + 1, :]
    q = q.astype(F32)
    o = _dot_nt((q * jnp.exp(b)).astype(BF16), st.astype(BF16))
    a = jnp.zeros((CHUNK, CHUNK), F32)
    for l, d in enumerate(CROSS_SIZES):
        e = jnp.exp(-jnp.abs(b - _block_reference(b, d, reverse)))
        pg = _dot_nt((q * e).astype(BF16), (k * e).astype(BF16))
        a = jnp.where(lev_ref[...] == l, pg, a)
    e = jnp.clip(b - _block_reference(b, DIAG_SIZE, reverse), -EXP_CLAMP, EXP_CLAMP)
    pg = _dot_nt((q * jnp.exp(e)).astype(BF16), (k * jnp.exp(-e)).astype(BF16))
    a = jnp.where(lev_ref[...] == len(CROSS_SIZES), pg, a)
    v16 = v.astype(BF16)
    o = o + _dot(a.astype(BF16), v16)
    k_dec = (k * jnp.exp(b_last - b)).astype(BF16)
    st_new = st * jnp.exp(b_last) + _dot_tn(v16, k_dec)
    return o, st_new


def _recurrence(q_s, k_s, v_s, lf_s, o_dst, st_ref, tri_ref, lev_ref, reverse):
    n_chunks = q_s.shape[0] // CHUNK
    for c in (range(n_chunks - 1, -1, -1) if reverse else range(n_chunks)):
        rows = slice(c * CHUNK, (c + 1) * CHUNK)
        lf_hi, lf_lo = _split2(lf_s[rows, :])
        lf_s[rows, :] = _dot(tri_ref[...], lf_hi) + _dot(tri_ref[...], lf_lo)
        for hd in range(N_HEADS):
            cols = slice(hd * HEAD_DIM, (hd + 1) * HEAD_DIM)
            o, st_new = _chunk_head(q_s[rows, cols], k_s[rows, cols], v_s[rows, cols],
                                    lf_s[rows, cols], st_ref[hd], lev_ref, reverse)
            st_ref[hd] = st_new
            o_dst[rows, cols] = o


def _norm_project(x_ref, mod_ref, g_ref, w_ref):
    x = x_ref[0]
    mod = mod_ref[0]
    h = _modulated_norm(x, g_ref[...], mod[1:2], mod[0:1])
    return x, mod, _dot(h.astype(BF16), w_ref[...])


def _fill_gates(z, lbl_ref, k_s, lf_s):
    lf, k = _forget_gate(z, _lower_bound(lbl_ref[0]))
    lf_s[...] = lf
    k_s[...] = k


def _mixer_bwd_kernel(x_ref, mod_ref, g_ref, lbl_ref, w_ref, tri_ref, lev_ref, ob_ref, uc_ref, qv_ref,
                      q_s, k_s, v_s, lf_s, st_ref):
    @pl.when(pl.program_id(1) == 0)
    def _():
        st_ref[...] = jnp.zeros_like(st_ref)

    _, _, u = _norm_project(x_ref, mod_ref, g_ref, w_ref)
    q = _silu(u[:, 0:KEY_W])
    v = u[:, 2 * KEY_W:3 * KEY_W]
    q_s[...] = q
    v_s[...] = v
    qv_ref[0, :, 0:KEY_W] = q.astype(BF16)
    qv_ref[0, :, KEY_W:2 * KEY_W] = v.astype(BF16)
    _fill_gates(u[:, KEY_W:2 * KEY_W], lbl_ref, k_s, lf_s)
    uc_ref[0] = u[:, 3 * KEY_W:3 * KEY_W + CONV_W] * u[:, 3 * KEY_W + CONV_W:]
    _recurrence(q_s, k_s, v_s, lf_s, ob_ref.at[0], st_ref, tri_ref, lev_ref, reverse=True)


def _mixer_fwd_kernel(x_ref, mod_ref, g_ref, lbl_ref, w_ref, tri_ref, lev_ref,
                      qv_ref, ob_ref, uc_ref, ucp_ref, ucn_ref, ng_ref, cw_ref, wo_ref, g2_ref, rw_ref,
                      x1_ref, h2_ref, aff_ref,
                      k_s, lf_s, o_s, st_ref):
    i = pl.program_id(1)
    n_i = pl.num_programs(1)

    @pl.when(i == 0)
    def _():
        st_ref[...] = jnp.zeros_like(st_ref)

    x, mod, u = _norm_project(x_ref, mod_ref, g_ref, w_ref)
    _fill_gates(u[:, 0:KEY_W], lbl_ref, k_s, lf_s)
    og = u[:, KEY_W:2 * KEY_W]
    cb = u[:, 2 * KEY_W:2 * KEY_W + CONV_W]
    _recurrence(qv_ref.at[0, :, pl.ds(0, KEY_W)], k_s, qv_ref.at[0, :, pl.ds(KEY_W, KEY_W)], lf_s,
                o_s, st_ref, tri_ref, lev_ref, reverse=False)

    o = o_s[...] + ob_ref[0]
    ng = ng_ref[...]
    heads = []
    for hd in range(N_HEADS):
        oh = o[:, hd * HEAD_DIM:(hd + 1) * HEAD_DIM]
        ms = jnp.mean(oh * oh, axis=-1, keepdims=True)
        heads.append(oh * lax.rsqrt(ms + RMS_EPS) * ng)
    ya = jnp.concatenate(heads, axis=1) * _silu(og)

    uc = uc_ref[0]
    row = lax.broadcasted_iota(I32, (MIX_ROWS, 1), 0)
    prev_row = jnp.where(i == 0, 0.0, ucp_ref[0][7:8, :])
    next_row = jnp.where(i == n_i - 1, 0.0, ucn_ref[0][0:1, :])
    u_prev = jnp.where(row == 0, prev_row, pltpu.roll(uc, 1, 0))
    u_next = jnp.where(row == MIX_ROWS - 1, next_row, pltpu.roll(uc, MIX_ROWS - 1, 0))
    cw = cw_ref[...]
    yb = cb * (cw[0:1] * u_prev + cw[1:2] * uc + cw[2:3] * u_next)

    y = jnp.concatenate([ya, yb], axis=1).astype(BF16)
    x1 = x + mod[2:3] * _dot(y, wo_ref[...])
    x1_ref[0] = x1
    h2 = _modulated_norm(x1, g2_ref[...], mod[4:5], mod[3:4])

    h_hi, h_lo = _split2(h2)
    two = _dot(h_hi, rw_ref[...])
    logits = two[:, 0:LANES] + two[:, LANES:2 * LANES] + _dot(h_lo, rw_ref[:, 0:LANES])
    lane = lax.broadcasted_iota(I32, logits.shape, 1)
    logits = jnp.where(lane < N_EXPERTS, logits, -jnp.inf)
    ex = jnp.exp(logits - jnp.max(logits, axis=-1, keepdims=True))
    aff = ex / jnp.sum(ex, axis=-1, keepdims=True)
    for c in range(ROW_TILES - 1):
        h2_ref[pl.ds(c, MIX_ROWS, stride=ROW_TILES), :] = h2[:, c * LANES:(c + 1) * LANES]
    h2_ref[pl.ds(ROW_TILES - 1, MIX_ROWS, stride=ROW_TILES), :] = aff
    aff_ref[...] = aff.T[0:N_EXPERTS, :]


def _mixer(x, mod, g_mix, lb_logits, w_b, w_f, norm_g, conv_w, w_out, g_ffn, rw_cat):
    bsz, t_len, d = x.shape
    n_i = t_len // MIX_ROWS
    r8 = MIX_ROWS // 8
    cparams = pltpu.CompilerParams(
        dimension_semantics=("arbitrary", "arbitrary"), vmem_limit_bytes=VMEM_LIMIT_BYTES)
    rec_scratch = [pltpu.VMEM((MIX_ROWS, KEY_W), F32) for _ in range(4)]
    state = pltpu.VMEM((N_HEADS, HEAD_DIM, HEAD_DIM), F32)
    row_block = lambda width, imap: pl.BlockSpec((1, MIX_ROWS, width), imap)
    const2 = lambda shape: pl.BlockSpec(shape, lambda b, i: (0, 0))
    rev = lambda b, i: (b, n_i - 1 - i, 0)
    fwd = lambda b, i: (b, i, 0)
    n_lb = lb_logits.shape[1]
    sq = (CHUNK, CHUNK)

    o_b, uc, qv = pl.pallas_call(
        _mixer_bwd_kernel,
        out_shape=(jax.ShapeDtypeStruct((bsz, t_len, KEY_W), F32),
                   jax.ShapeDtypeStruct((bsz, t_len, CONV_W), F32),
                   jax.ShapeDtypeStruct((bsz, t_len, 2 * KEY_W), BF16)),
        grid=(bsz, n_i),
        in_specs=[
            row_block(d, rev),
            pl.BlockSpec((1, 6, d), lambda b, i: (b, 0, 0)),
            const2((1, d)),
            pl.BlockSpec((1, n_lb, KEY_W), lambda b, i: (1, 0, 0)),
            const2(w_b.shape),
            const2(sq), const2(sq),
        ],
        out_specs=(row_block(KEY_W, rev), row_block(CONV_W, rev), row_block(2 * KEY_W, rev)),
        scratch_shapes=rec_scratch + [state],
        compiler_params=cparams,
        name="mixer_bwd",
    )(x, mod, g_mix, lb_logits, w_b, _cumsum_table(True), _level_table(True))

    x1, h2, aff_t = pl.pallas_call(
        _mixer_fwd_kernel,
        out_shape=(jax.ShapeDtypeStruct((bsz, t_len, d), F32),
                   jax.ShapeDtypeStruct((bsz * t_len * ROW_TILES, LANES), F32),
                   jax.ShapeDtypeStruct((N_EXPERTS, bsz * t_len), F32)),
        grid=(bsz, n_i),
        in_specs=[
            row_block(d, fwd),
            pl.BlockSpec((1, 6, d), lambda b, i: (b, 0, 0)),
            const2((1, d)),
            pl.BlockSpec((1, n_lb, KEY_W), lambda b, i: (0, 0, 0)),
            const2(w_f.shape),
            const2(sq), const2(sq),
            row_block(2 * KEY_W, fwd),
            row_block(KEY_W, fwd),
            row_block(CONV_W, fwd),
            pl.BlockSpec((1, 8, CONV_W), lambda b, i: (b, jnp.maximum(i * r8 - 1, 0), 0)),
            pl.BlockSpec((1, 8, CONV_W), lambda b, i: (b, jnp.minimum((i + 1) * r8, n_i * r8 - 1), 0)),
            const2((1, HEAD_DIM)),
            const2(conv_w.shape),
            const2(w_out.shape),
            const2((1, d)),
            const2(rw_cat.shape),
        ],
        out_specs=(row_block(d, fwd),
                   pl.BlockSpec((MIX_ROWS * ROW_TILES, LANES), lambda b, i: (b * n_i + i, 0)),
                   pl.BlockSpec((N_EXPERTS, MIX_ROWS), lambda b, i: (0, b * n_i + i))),
        scratch_shapes=rec_scratch[:3] + [state],
        compiler_params=cparams,
        name="mixer_fwd",
    )(x, mod, g_mix, lb_logits, w_f, _cumsum_table(False), _level_table(False),
      qv, o_b, uc, uc, uc, norm_g, conv_w, w_out, g_ffn, rw_cat)
    return x1, h2, aff_t


def _threshold_kernel(a_ref, thr_ref, need_ref, *, cap):
    bits = pltpu.bitcast(a_ref[...], I32)

    def count(m):
        return jnp.sum(jnp.where(m, 1.0, 0.0), axis=1, keepdims=True)

    def body(it, thr):
        cand = thr | jnp.left_shift(jnp.int32(1), 30 - it)
        return jnp.where(count(bits >= cand) >= cap, cand, thr)

    thr = lax.fori_loop(0, 31, body, jnp.zeros((N_EXPERTS, 1), I32))
    need = cap - count(bits > thr)
    thr_ref[...] = jnp.broadcast_to(thr, thr_ref.shape)
    need_ref[...] = jnp.broadcast_to(need, need_ref.shape)


def _slot_index_kernel(a_ref, thr_ref, need_ref, idx_ref, pos_ref, offs_ref, *, cap, n_tiles):
    e = pl.program_id(0)
    bits = pltpu.bitcast(a_ref[0], I32)
    thr = thr_ref[0]
    need = need_ref[0]
    lane = lax.broadcasted_iota(I32, (n_tiles, LANES), 1)
    upper = jnp.where(lax.broadcasted_iota(I32, (LANES, LANES), 0)
                      <= lax.broadcasted_iota(I32, (LANES, LANES), 1), 1.0, 0.0).astype(BF16)
    below = jnp.where(lax.broadcasted_iota(I32, (n_tiles, n_tiles), 1)
                      < lax.broadcasted_iota(I32, (n_tiles, n_tiles), 0), 1.0, 0.0).astype(BF16)

    def running_count(m):
        within = _dot(jnp.where(m, 1.0, 0.0).astype(BF16), upper)
        total = jnp.broadcast_to(within[:, LANES - 1:LANES], within.shape)
        before = _dot(below, total.astype(BF16))
        return within, before, total

    gt = bits > thr
    eq = bits == thr
    w_eq, b_eq, _ = running_count(eq)
    sel = gt | (eq & ((w_eq + b_eq) <= need))
    within, before, total = running_count(sel)
    cum = within + before
    pos_ref[0] = jnp.where(sel, cum - 1.0 + (e * cap).astype(F32), -1.0).astype(I32)
    offs_ref[0] = before.astype(I32)

    slot = lax.broadcasted_iota(I32, (n_tiles, cap), 1).astype(F32)
    lo = jnp.broadcast_to(before[:, 0:1], (n_tiles, cap))
    hi = jnp.broadcast_to((before + total)[:, 0:1], (n_tiles, cap))
    in_tile = jnp.where((lo <= slot) & (slot < hi), 1.0, 0.0).astype(BF16)
    b_hi = jnp.floor(before * (1.0 / 256.0))
    b_lo = before - 256.0 * b_hi
    tile_id = lax.broadcasted_iota(I32, (n_tiles, LANES), 0).astype(F32)
    extras = jnp.where(lane == 0, b_hi, jnp.where(lane == 1, b_lo, jnp.where(lane == 2, tile_id, 0.0)))
    table = jnp.concatenate([within, extras], axis=1).T.astype(BF16)
    picked = _dot(table, in_tile)
    slot_row = slot[0:1, :]
    before_p = picked[LANES:LANES + 1, :] * 256.0 + picked[LANES + 1:LANES + 2, :]
    tile_p = picked[LANES + 2:LANES + 3, :]
    lane_p = jnp.sum(jnp.where(picked[0:LANES, :] <= slot_row - before_p, 1.0, 0.0),
                     axis=0, keepdims=True)
    idx_ref[0] = (tile_p * float(LANES) + lane_p).astype(I32) * ROW_TILES


def _route(aff_t, cap):
    n = aff_t.shape[1]
    n_tiles = n // LANES
    cparams1 = pltpu.CompilerParams(dimension_semantics=("arbitrary",),
                                    vmem_limit_bytes=VMEM_LIMIT_BYTES)
    thr, need = pl.pallas_call(
        functools.partial(_threshold_kernel, cap=cap),
        out_shape=(jax.ShapeDtypeStruct((N_EXPERTS, LANES), I32),
                   jax.ShapeDtypeStruct((N_EXPERTS, LANES), F32)),
        grid=(1,),
        in_specs=[pl.BlockSpec((N_EXPERTS, n), lambda i: (0, 0))],
        out_specs=(pl.BlockSpec((N_EXPERTS, LANES), lambda i: (0, 0)),
                   pl.BlockSpec((N_EXPERTS, LANES), lambda i: (0, 0))),
        compiler_params=cparams1,
        name="route_threshold",
    )(aff_t)
    per_expert = lambda width: pl.BlockSpec((1, 1, width), lambda e: (e, 0, 0))
    tiles = pl.BlockSpec((1, n_tiles, LANES), lambda e: (e, 0, 0))
    idx, pos, offs = pl.pallas_call(
        functools.partial(_slot_index_kernel, cap=cap, n_tiles=n_tiles),
        out_shape=(jax.ShapeDtypeStruct((N_EXPERTS, 1, cap), I32),
                   jax.ShapeDtypeStruct((N_EXPERTS, n_tiles, LANES), I32),
                   jax.ShapeDtypeStruct((N_EXPERTS, n_tiles, LANES), I32)),
        grid=(N_EXPERTS,),
        in_specs=[tiles, per_expert(LANES), per_expert(LANES)],
        out_specs=(per_expert(cap), tiles, tiles),
        compiler_params=cparams1,
        name="route_slots",
    )(aff_t.reshape(N_EXPERTS, n_tiles, LANES),
      thr.reshape(N_EXPERTS, 1, LANES), need.reshape(N_EXPERTS, 1, LANES))
    return idx, pos, offs


def _ffn_kernel(idx_hbm, h_hbm, wg_ref, wu_ref, wd_ref, y_ref,
                idx_s, buf, sem_i, sem_g, *, n_tiles):
    e = pl.program_id(0)
    pair = e * pl.num_programs(1) + pl.program_id(1)
    last = n_tiles - 1
    tile_rows = FFN_ROWS * ROW_TILES

    def idx_copy(t, sl):
        return pltpu.make_async_copy(idx_hbm.at[jnp.minimum(t, last)], idx_s.at[sl], sem_i.at[sl])

    def row_copy(sl, r, row0):
        return pltpu.make_async_copy(h_hbm.at[pl.ds(row0, ROW_TILES), :],
                                     buf.at[pl.ds((sl * FFN_ROWS + r) * ROW_TILES, ROW_TILES), :],
                                     sem_g.at[sl])

    def all_rows(sl):
        return pltpu.make_async_copy(h_hbm.at[pl.ds(0, tile_rows), :],
                                     buf.at[pl.ds(sl * tile_rows, tile_rows), :], sem_g.at[sl])

    @pl.when(pair == 0)
    def _():
        idx_copy(0, 0).start()
        idx_copy(0, 0).wait()

        def body(r, carry):
            pltpu.make_async_copy(h_hbm.at[pl.ds(idx_s[0, r], ROW_TILES), :],
                                  buf.at[pl.ds(r * ROW_TILES, ROW_TILES), :], sem_g.at[0]).start()
            return carry
        lax.fori_loop(0, FFN_ROWS, body, 0)
        idx_copy(1, 1).start()

    lane = lax.broadcasted_iota(I32, (FFN_ROWS, LANES), 1)

    def tile(t, sl):
        other = 1 - sl
        idx_copy(t + 1, other).wait()
        for r in range(FFN_ROWS):
            row_copy(other, r, idx_s[other, r]).start()
        idx_copy(t + 2, sl).start()
        all_rows(sl).wait()

        parts = [buf[pl.ds(sl * tile_rows + c, FFN_ROWS, stride=ROW_TILES), :] for c in range(ROW_TILES)]
        gate = jnp.sum(jnp.where(lane == e, parts[-1], 0.0), axis=-1, keepdims=True)
        x16 = jnp.concatenate([p.astype(BF16) for p in parts[:-1]], axis=1)
        hid = _silu(_dot(x16, wg_ref[0])) * _dot(x16, wu_ref[0])
        y_hi, y_lo = _split2(_dot(hid.astype(BF16), wd_ref[0]) * gate)
        rows = slice(sl * FFN_ROWS, (sl + 1) * FFN_ROWS)
        y_ref[rows, 0:D_MODEL] = y_hi
        y_ref[rows, D_MODEL:2 * D_MODEL] = y_lo

    tile(2 * pair, 0)
    tile(2 * pair + 1, 1)

    @pl.when(2 * pair + 1 == last)
    def _():
        idx_copy(last + 2, 1).wait()
        all_rows(0).wait()


def _expert_ffn(idx, h_rows, w_gate, w_up, w_down):
    cap = idx.shape[-1]
    n_j = cap // (2 * FFN_ROWS)
    n_tiles = 2 * N_EXPERTS * n_j
    d, ff = w_gate.shape[1], w_gate.shape[2]
    per_expert = lambda shape: pl.BlockSpec((1,) + shape, lambda e, j: (e, 0, 0))
    return pl.pallas_call(
        functools.partial(_ffn_kernel, n_tiles=n_tiles),
        out_shape=jax.ShapeDtypeStruct((N_EXPERTS * cap, 2 * d), BF16),
        grid=(N_EXPERTS, n_j),
        in_specs=[
            pl.BlockSpec(memory_space=pl.ANY),
            pl.BlockSpec(memory_space=pl.ANY),
            per_expert((d, ff)), per_expert((d, ff)), per_expert((ff, d)),
        ],
        out_specs=pl.BlockSpec((2 * FFN_ROWS, 2 * d), lambda e, j: (e * n_j + j, 0)),
        scratch_shapes=[
            pltpu.SMEM((2, FFN_ROWS), I32),
            pltpu.VMEM((2 * FFN_ROWS * ROW_TILES, LANES), F32),
            pltpu.SemaphoreType.DMA((2,)),
            pltpu.SemaphoreType.DMA((2,)),
        ],
        compiler_params=pltpu.CompilerParams(
            dimension_semantics=("arbitrary", "arbitrary"), vmem_limit_bytes=VMEM_LIMIT_BYTES),
        name="expert_ffn",
    )(idx.reshape(n_tiles, FFN_ROWS), h_rows, w_gate, w_up, w_down)


def _combine_kernel(offs_ref, x1_ref, pos_ref, mod_ref, g_ref, y_hbm, out_ref,
                    win, acc, sem, *, cap, total_rows):
    i = pl.program_id(0)
    slot = i % 2
    lane = lax.broadcasted_iota(I32, (1, LANES), 1)
    d = out_ref.shape[1]

    def first_rows(tile):
        return [((e * cap + offs_ref[e, tile]) // BF16_ROWS) * BF16_ROWS for e in range(N_EXPERTS)]

    def window_copies(want_rows, sl):
        copies = []
        for e in range(N_EXPERTS):
            row0 = pl.multiple_of(jnp.minimum(want_rows[e], total_rows - COMB_WIN), BF16_ROWS)
            copies.append(pltpu.make_async_copy(y_hbm.at[pl.ds(row0, COMB_WIN), :],
                                                win.at[sl, pl.ds(e * COMB_WIN, COMB_WIN), :], sem.at[sl]))
        return copies

    def wait_windows(sl):
        for e in range(N_EXPERTS):
            pltpu.make_async_copy(y_hbm.at[pl.ds(0, COMB_WIN), :],
                                  win.at[sl, pl.ds(e * COMB_WIN, COMB_WIN), :], sem.at[sl]).wait()

    def spread_rows(one_hot, sl):
        both = _dot(one_hot, win[sl])
        return both[:, 0:d] + both[:, d:2 * d]

    def lane_vector(values):
        vec = jnp.zeros((1, LANES), I32)
        for e, val in enumerate(values):
            vec = jnp.where(lane == e, val, vec)
        return vec

    starts = first_rows(i)

    @pl.when(i == 0)
    def _():
        for cp in window_copies(starts, 0):
            cp.start()

    @pl.when(i + 1 < pl.num_programs(0))
    def _():
        for cp in window_copies(first_rows(i + 1), 1 - slot):
            cp.start()

    span = functools.reduce(
        jnp.maximum, [e * cap + offs_ref[e, i + 1] - starts[e] for e in range(N_EXPERTS)])
    n_rounds = (span + COMB_WIN - 1) // COMB_WIN

    expand = jnp.where(
        lax.broadcasted_iota(I32, (LANES, N_EXPERTS * COMB_WIN), 0)
        == lax.broadcasted_iota(I32, (LANES, N_EXPERTS * COMB_WIN), 1) // COMB_WIN, 1.0, 0.0).astype(BF16)
    in_win = (lax.broadcasted_iota(I32, (1, N_EXPERTS * COMB_WIN), 1) % COMB_WIN).astype(F32)
    pos = pos_ref[...]

    def spread(rd):
        want_rows = [s + rd * COMB_WIN for s in starts]
        want = lane_vector(want_rows)
        base = jnp.minimum(want, total_rows - COMB_WIN)
        mine = (pos >= want) & (pos < want + COMB_WIN)
        rel = jnp.where(mine, pos - base, -1).astype(F32).astype(BF16)
        one_hot = jnp.where(_dot(rel, expand) == in_win, 1.0, 0.0).astype(BF16)
        return want_rows, one_hot

    _, one_hot = spread(0)
    wait_windows(slot)
    acc[...] = spread_rows(one_hot, slot)

    def round_body(rd, carry):
        want_rows, one_hot = spread(rd)
        for cp in window_copies(want_rows, slot):
            cp.start()
        wait_windows(slot)
        acc[...] += spread_rows(one_hot, slot)
        return carry

    lax.fori_loop(1, n_rounds, round_body, 0)

    mod = mod_ref[0]
    y = x1_ref[...] + mod[5:6] * acc[...]
    ms = jnp.mean(y * y, axis=-1, keepdims=True)
    out_ref[...] = y * lax.rsqrt(ms + RMS_EPS) * g_ref[...]


def _combine(offs_tab, x1_rows, pos_rows, mod, final_g, y_rows, cap, t_len):
    n, d = x1_rows.shape
    tiles_per_seq = t_len // COMB_ROWS
    total_rows = y_rows.shape[0]
    window = pltpu.VMEM((2, N_EXPERTS * COMB_WIN, 2 * d), BF16)
    return pl.pallas_call(
        functools.partial(_combine_kernel, cap=cap, total_rows=total_rows),
        out_shape=jax.ShapeDtypeStruct((n, d), F32),
        grid_spec=pltpu.PrefetchScalarGridSpec(
            num_scalar_prefetch=1,
            grid=(n // COMB_ROWS,),
            in_specs=[
                pl.BlockSpec((COMB_ROWS, d), lambda i, offs: (i, 0)),
                pl.BlockSpec((COMB_ROWS, LANES), lambda i, offs: (i, 0)),
                pl.BlockSpec((1, 6, d), lambda i, offs: (i // tiles_per_seq, 0, 0)),
                pl.BlockSpec((1, d), lambda i, offs: (0, 0)),
                pl.BlockSpec(memory_space=pl.ANY),
            ],
            out_specs=pl.BlockSpec((COMB_ROWS, d), lambda i, offs: (i, 0)),
            scratch_shapes=[window, pltpu.VMEM((COMB_ROWS, d), F32), pltpu.SemaphoreType.DMA((2,))],
        ),
        compiler_params=pltpu.CompilerParams(
            dimension_semantics=("arbitrary",), vmem_limit_bytes=VMEM_LIMIT_BYTES),
        name="moe_combine",
    )(offs_tab, x1_rows, pos_rows, mod, final_g, y_rows)


def _trunk(x, mod, params):
    (g_mix, g_ffn, w_b, w_f, lb_logits, norm_g, conv_w, w_out, rw_cat,
     w_gate, w_up, w_down, final_g) = params
    bsz, t_len, d = x.shape
    n = bsz * t_len
    cap = CAPACITY_FACTOR * n // N_EXPERTS

    x1, h2, aff_t = _mixer(x, mod, g_mix, lb_logits, w_b, w_f, norm_g, conv_w, w_out, g_ffn, rw_cat)
    idx, pos, offs = _route(aff_t, cap)
    y_rows = _expert_ffn(idx, h2, w_gate, w_up, w_down)

    pos_rows = jnp.pad(pos.reshape(N_EXPERTS, n).T, ((0, 0), (0, LANES - N_EXPERTS)), constant_values=-1)
    step = COMB_ROWS // LANES
    offs_tab = jnp.concatenate(
        [offs[:, ::step, 0], jnp.full((N_EXPERTS, 1), cap, I32)], axis=1)
    y = _combine(offs_tab, x1.reshape(n, d), pos_rows, mod, final_g, y_rows, cap, t_len)
    return y.reshape(bsz, t_len, d)


def kernel(x_prompt, x_sample, c_prompt, c_sample, ada_w, ada_b, norm_mix_g, norm_ffn_g, w_in,
           hgrn_lb_logits, hgrn_out_norm_g, conv_w, w_out, router_w, w_gate, w_up, w_down,
           final_norm_g):
    assert ada_w.shape[0] == 1, "one encoder layer"
    d = D_MODEL
    q, zf, zb, v, og, cb, cc, ch = jnp.split(
        w_in[0], [512, 1024, 1536, 2048, 2560, 3072, 3584], axis=1)
    w_b = jnp.concatenate([q, zb, v, cc, ch], axis=1).astype(BF16)
    w_f = jnp.concatenate([zf, og, cb], axis=1).astype(BF16)
    rw = jnp.pad(router_w[0], ((0, 0), (0, LANES - N_EXPERTS)))
    rw_hi = rw.astype(BF16)
    rw_cat = jnp.concatenate([rw_hi, (rw - rw_hi.astype(F32)).astype(BF16)], axis=1)
    params = (norm_mix_g[0].reshape(1, d), norm_ffn_g[0].reshape(1, d), w_b, w_f, hgrn_lb_logits,
              hgrn_out_norm_g[0].reshape(1, HEAD_DIM), conv_w[0], w_out[0].astype(BF16), rw_cat,
              w_gate[0].astype(BF16), w_up[0].astype(BF16), w_down[0].astype(BF16),
              final_norm_g.reshape(1, d))

    nb_p, nb_s = c_prompt.shape[0], c_sample.shape[0]
    c_rows = jnp.concatenate(
        [c_prompt, c_sample, jnp.zeros((8 - nb_p - nb_s, d), F32)], axis=0)
    mod = _modulation(c_rows, ada_w[0], ada_b[0])
    mod_p = mod[:nb_p].reshape(nb_p, 6, d)
    mod_s = mod[nb_p:nb_p + nb_s].reshape(nb_s, 6, d)
    return (_trunk(x_prompt, mod_p, params), _trunk(x_sample, mod_s, params))
```

```python
import functools

import numpy as np
import jax
import jax.numpy as jnp
from jax import lax
from jax.experimental import pallas as pl
from jax.experimental.pallas import tpu as pltpu

F32 = jnp.float32
BF16 = jnp.bfloat16
I32 = jnp.int32

D_MODEL = 1024
N_HEADS = 4
HEAD_DIM = 128
KEY_W = N_HEADS * HEAD_DIM
CONV_W = D_MODEL - KEY_W
N_EXPERTS = 16
CAPACITY_FACTOR = 2
RMS_EPS = 1e-6

LANES = 128
BF16_ROWS = 16
VMEM_LIMIT_BYTES = 56 * 1024 * 1024

CHUNK = 256
HALF = CHUNK // 2
SUB_SIZES = (128, 64, 32)
DIAG_SIZE = 16
EXP2_CLAMP = 115.0

MIX_ROWS = 512
ROW_TILES = D_MODEL // LANES + 1
FFN_ROWS = 256
COMB_ROWS = 256
COMB_WIN = 64


def _dot(a, b):
    return jnp.dot(a, b, preferred_element_type=F32)


def _dot_nt(a, b):
    return lax.dot_general(a, b, (((1,), (1,)), ((), ())), preferred_element_type=F32)


def _dot_tn(a, b):
    return lax.dot_general(a, b, (((0,), (0,)), ((), ())), preferred_element_type=F32)


def _split2(x):
    hi = x.astype(BF16)
    lo = (x - hi.astype(F32)).astype(BF16)
    return hi, lo


def _silu(x):
    return x / (1.0 + jnp.exp(-x))


def _modulated_norm(x, gain, scale, shift):
    ms = jnp.mean(x * x, axis=-1, keepdims=True)
    return x * lax.rsqrt(ms + RMS_EPS) * gain * (1.0 + scale) + shift


def _modulation_kernel(c_ref, w_ref, b_ref, o_ref):
    w_hi, w_lo = _split2(w_ref[...])
    s_hi, s_lo = _split2(_silu(c_ref[...]))
    o_ref[...] = _dot(s_hi, w_hi) + _dot(s_lo, w_hi) + _dot(s_hi, w_lo) + b_ref[...]


def _modulation(c_rows, ada_w, ada_b):
    rows, d = c_rows.shape
    cols = ada_w.shape[1]
    bn = 1024
    return pl.pallas_call(
        _modulation_kernel,
        out_shape=jax.ShapeDtypeStruct((rows, cols), F32),
        grid=(cols // bn,),
        in_specs=[
            pl.BlockSpec((rows, d), lambda j: (0, 0)),
            pl.BlockSpec((d, bn), lambda j: (0, j)),
            pl.BlockSpec((1, bn), lambda j: (0, j)),
        ],
        out_specs=pl.BlockSpec((rows, bn), lambda j: (0, j)),
        compiler_params=pltpu.CompilerParams(
            dimension_semantics=("arbitrary",), vmem_limit_bytes=VMEM_LIMIT_BYTES),
        name="modulation",
    )(c_rows, ada_w, ada_b.reshape(1, cols))


def _lower_bound(logits):
    e = jnp.exp(logits - jnp.max(logits, axis=0, keepdims=True))
    return e[0:1] / jnp.sum(e, axis=0, keepdims=True)


def _forget_gate(z, lb):
    e = jnp.exp(-jnp.abs(z))
    big = 1.0 / (1.0 + e)
    small = e * big
    pos = z >= 0.0
    f = lb + (1.0 - lb) * jnp.where(pos, big, small)
    k = (1.0 - lb) * jnp.where(pos, small, big)
    return jnp.log2(f), k


def _level_table(reverse):
    t = np.arange(HALF)[:, None]
    s = np.arange(HALF)[None, :]
    if reverse:
        t, s = s, t
    lev = np.full((HALF, HALF), -1, np.int32)
    for l, d in enumerate(SUB_SIZES):
        lev[(t // d == s // d) & (t % d >= d // 2) & (s % d < d // 2)] = l
    lev[(t // DIAG_SIZE == s // DIAG_SIZE) & (s <= t)] = len(SUB_SIZES)
    return jnp.asarray(lev)


def _cumsum_table(reverse):
    t = np.arange(CHUNK)[:, None]
    s = np.arange(CHUNK)[None, :]
    keep = (s >= t) if reverse else (s <= t)
    return jnp.asarray(keep.astype(np.float32)).astype(BF16)


def _cross_factor(b, d, reverse):
    b4 = b.reshape(CHUNK // d, 2, d // 2, HEAD_DIM)
    lo, hi = b4[:, 0], b4[:, 1]
    if reverse:
        r = hi[:, 0:1, :]
        x = jnp.stack([lo - r, r - hi], axis=1)
    else:
        r = lo[:, d // 2 - 1:d // 2, :]
        x = jnp.stack([r - lo, hi - r], axis=1)
    return jnp.exp2(x).reshape(CHUNK, HEAD_DIM)


def _diag_exponent(b, reverse):
    h = DIAG_SIZE // 2 if reverse else DIAG_SIZE // 2 - 1
    b3 = b.reshape(CHUNK // DIAG_SIZE, DIAG_SIZE, HEAD_DIM)
    x = (b3 - b3[:, h:h + 1, :]).reshape(CHUNK, HEAD_DIM)
    return jnp.clip(x, -EXP2_CLAMP, EXP2_CLAMP)


def _chunk_head(q, k, v, b, st, lev_ref, reverse):
    last = 0 if reverse else CHUNK - 1
    b_last = b[last:last + 1, :]
    q = q.astype(F32)
    v16 = v.astype(BF16)
    o = _dot_nt((q * jnp.exp2(b)).astype(BF16), st.astype(BF16))

    e = _cross_factor(b, CHUNK, reverse)
    p, g = (q * e).astype(BF16), (k * e).astype(BF16)
    lo_rows, hi_rows = slice(0, HALF), slice(HALF, CHUNK)
    out_rows, src_rows = (lo_rows, hi_rows) if reverse else (hi_rows, lo_rows)
    a_cross = _dot_nt(p[out_rows], g[src_rows]).astype(BF16)

    operands = []
    for d in SUB_SIZES:
        e = _cross_factor(b, d, reverse)
        operands.append(((q * e).astype(BF16), (k * e).astype(BF16)))
    x = _diag_exponent(b, reverse)
    operands.append(((q * jnp.exp2(x)).astype(BF16), (k * jnp.exp2(-x)).astype(BF16)))
    a_half = []
    for rows in (lo_rows, hi_rows):
        a = jnp.zeros((HALF, HALF), F32)
        for l, (p, g) in enumerate(operands):
            a = jnp.where(lev_ref[...] == l, _dot_nt(p[rows], g[rows]), a)
        a_half.append(a.astype(BF16))
    if reverse:
        o_lo = _dot(jnp.concatenate([a_half[0], a_cross], axis=1), v16)
        o_hi = _dot(a_half[1], v16[hi_rows])
    else:
        o_lo = _dot(a_half[0], v16[lo_rows])
        o_hi = _dot(jnp.concatenate([a_cross, a_half[1]], axis=1), v16)
    o = o + jnp.concatenate([o_lo, o_hi], axis=0)

    k_dec = (k * jnp.exp2(b_last - b)).astype(BF16)
    st_new = st * jnp.exp2(b_last) + _dot_tn(v16, k_dec)
    return o, st_new


def _recurrence(q_s, k_s, v_s, lf_s, o_dst, st_ref, tri_ref, lev_ref, reverse):
    n_chunks = q_s.shape[0] // CHUNK
    for c in (range(n_chunks - 1, -1, -1) if reverse else range(n_chunks)):
        rows = slice(c * CHUNK, (c + 1) * CHUNK)
        lf_hi, lf_lo = _split2(lf_s[rows, :])
        lf_s[rows, :] = _dot(tri_ref[...], lf_hi) + _dot(tri_ref[...], lf_lo)
        for hd in range(N_HEADS):
            cols = slice(hd * HEAD_DIM, (hd + 1) * HEAD_DIM)
            o, st_new = _chunk_head(q_s[rows, cols], k_s[rows, cols], v_s[rows, cols],
                                    lf_s[rows, cols], st_ref[hd], lev_ref, reverse)
            st_ref[hd] = st_new
            o_dst[rows, cols] = o


def _norm_project(x_ref, mod_ref, g_ref, w_ref):
    x = x_ref[0]
    mod = mod_ref[0]
    h = _modulated_norm(x, g_ref[...], mod[1:2], mod[0:1])
    return x, mod, _dot(h.astype(BF16), w_ref[...])


def _fill_gates(z, lbl_ref, k_s, lf_s):
    lf, k = _forget_gate(z, _lower_bound(lbl_ref[0]))
    lf_s[...] = lf
    k_s[...] = k


def _mixer_bwd_kernel(x_ref, mod_ref, g_ref, lbl_ref, w_ref, tri_ref, lev_ref, ob_ref, uc_ref, qv_ref,
                      q_s, k_s, v_s, lf_s, st_ref):
    @pl.when(pl.program_id(1) == 0)
    def _():
        st_ref[...] = jnp.zeros_like(st_ref)

    _, _, u = _norm_project(x_ref, mod_ref, g_ref, w_ref)
    q = _silu(u[:, 0:KEY_W])
    v = u[:, 2 * KEY_W:3 * KEY_W]
    q_s[...] = q
    v_s[...] = v
    qv_ref[0, :, 0:KEY_W] = q.astype(BF16)
    qv_ref[0, :, KEY_W:2 * KEY_W] = v.astype(BF16)
    _fill_gates(u[:, KEY_W:2 * KEY_W], lbl_ref, k_s, lf_s)
    uc_ref[0] = u[:, 3 * KEY_W:3 * KEY_W + CONV_W] * u[:, 3 * KEY_W + CONV_W:]
    _recurrence(q_s, k_s, v_s, lf_s, ob_ref.at[0], st_ref, tri_ref, lev_ref, reverse=True)


def _mixer_fwd_kernel(x_ref, mod_ref, g_ref, lbl_ref, w_ref, tri_ref, lev_ref,
                      qv_ref, ob_ref, uc_ref, ucp_ref, ucn_ref, ng_ref, cw_ref, wo_ref, g2_ref, rw_ref,
                      x1_ref, h2_ref, aff_ref,
                      k_s, lf_s, o_s, st_ref):
    i = pl.program_id(1)
    n_i = pl.num_programs(1)

    @pl.when(i == 0)
    def _():
        st_ref[...] = jnp.zeros_like(st_ref)

    x, mod, u = _norm_project(x_ref, mod_ref, g_ref, w_ref)
    _fill_gates(u[:, 0:KEY_W], lbl_ref, k_s, lf_s)
    og = u[:, KEY_W:2 * KEY_W]
    cb = u[:, 2 * KEY_W:2 * KEY_W + CONV_W]
    _recurrence(qv_ref.at[0, :, pl.ds(0, KEY_W)], k_s, qv_ref.at[0, :, pl.ds(KEY_W, KEY_W)], lf_s,
                o_s, st_ref, tri_ref, lev_ref, reverse=False)

    o = o_s[...] + ob_ref[0]
    ng = ng_ref[...]
    heads = []
    for hd in range(N_HEADS):
        oh = o[:, hd * HEAD_DIM:(hd + 1) * HEAD_DIM]
        ms = jnp.mean(oh * oh, axis=-1, keepdims=True)
        heads.append(oh * lax.rsqrt(ms + RMS_EPS) * ng)
    ya = jnp.concatenate(heads, axis=1) * _silu(og)

    uc = uc_ref[0]
    row = lax.broadcasted_iota(I32, (MIX_ROWS, 1), 0)
    prev_row = jnp.where(i == 0, 0.0, ucp_ref[0][7:8, :])
    next_row = jnp.where(i == n_i - 1, 0.0, ucn_ref[0][0:1, :])
    u_prev = jnp.where(row == 0, prev_row, pltpu.roll(uc, 1, 0))
    u_next = jnp.where(row == MIX_ROWS - 1, next_row, pltpu.roll(uc, MIX_ROWS - 1, 0))
    cw = cw_ref[...]
    yb = cb * (cw[0:1] * u_prev + cw[1:2] * uc + cw[2:3] * u_next)

    y = jnp.concatenate([ya, yb], axis=1).astype(BF16)
    x1 = x + mod[2:3] * _dot(y, wo_ref[...])
    x1_ref[0] = x1
    h2 = _modulated_norm(x1, g2_ref[...], mod[4:5], mod[3:4])

    h_hi, h_lo = _split2(h2)
    two = _dot(h_hi, rw_ref[...])
    logits = two[:, 0:LANES] + two[:, LANES:2 * LANES] + _dot(h_lo, rw_ref[:, 0:LANES])
    lane = lax.broadcasted_iota(I32, logits.shape, 1)
    logits = jnp.where(lane < N_EXPERTS, logits, -jnp.inf)
    ex = jnp.exp(logits - jnp.max(logits, axis=-1, keepdims=True))
    aff = ex / jnp.sum(ex, axis=-1, keepdims=True)
    for c in range(ROW_TILES - 1):
        h2_ref[pl.ds(c, MIX_ROWS, stride=ROW_TILES), :] = h2[:, c * LANES:(c + 1) * LANES]
    h2_ref[pl.ds(ROW_TILES - 1, MIX_ROWS, stride=ROW_TILES), :] = aff
    aff_ref[...] = aff.T[0:N_EXPERTS, :]


def _mixer(x, mod, g_mix, lb_logits, w_b, w_f, norm_g, conv_w, w_out, g_ffn, rw_cat):
    bsz, t_len, d = x.shape
    n_i = t_len // MIX_ROWS
    r8 = MIX_ROWS // 8
    cparams = pltpu.CompilerParams(
        dimension_semantics=("arbitrary", "arbitrary"), vmem_limit_bytes=VMEM_LIMIT_BYTES)
    rec_scratch = [pltpu.VMEM((MIX_ROWS, KEY_W), F32) for _ in range(4)]
    state = pltpu.VMEM((N_HEADS, HEAD_DIM, HEAD_DIM), F32)
    row_block = lambda width, imap: pl.BlockSpec((1, MIX_ROWS, width), imap)
    const2 = lambda shape: pl.BlockSpec(shape, lambda b, i: (0, 0))
    rev = lambda b, i: (b, n_i - 1 - i, 0)
    fwd = lambda b, i: (b, i, 0)
    n_lb = lb_logits.shape[1]

    o_b, uc, qv = pl.pallas_call(
        _mixer_bwd_kernel,
        out_shape=(jax.ShapeDtypeStruct((bsz, t_len, KEY_W), F32),
                   jax.ShapeDtypeStruct((bsz, t_len, CONV_W), F32),
                   jax.ShapeDtypeStruct((bsz, t_len, 2 * KEY_W), BF16)),
        grid=(bsz, n_i),
        in_specs=[
            row_block(d, rev),
            pl.BlockSpec((1, 6, d), lambda b, i: (b, 0, 0)),
            const2((1, d)),
            pl.BlockSpec((1, n_lb, KEY_W), lambda b, i: (1, 0, 0)),
            const2(w_b.shape),
            const2((CHUNK, CHUNK)), const2((HALF, HALF)),
        ],
        out_specs=(row_block(KEY_W, rev), row_block(CONV_W, rev), row_block(2 * KEY_W, rev)),
        scratch_shapes=rec_scratch + [state],
        compiler_params=cparams,
        name="mixer_bwd",
    )(x, mod, g_mix, lb_logits, w_b, _cumsum_table(True), _level_table(True))

    x1, h2, aff_t = pl.pallas_call(
        _mixer_fwd_kernel,
        out_shape=(jax.ShapeDtypeStruct((bsz, t_len, d), F32),
                   jax.ShapeDtypeStruct((bsz * t_len * ROW_TILES, LANES), F32),
                   jax.ShapeDtypeStruct((N_EXPERTS, bsz * t_len), F32)),
        grid=(bsz, n_i),
        in_specs=[
            row_block(d, fwd),
            pl.BlockSpec((1, 6, d), lambda b, i: (b, 0, 0)),
            const2((1, d)),
            pl.BlockSpec((1, n_lb, KEY_W), lambda b, i: (0, 0, 0)),
            const2(w_f.shape),
            const2((CHUNK, CHUNK)), const2((HALF, HALF)),
            row_block(2 * KEY_W, fwd),
            row_block(KEY_W, fwd),
            row_block(CONV_W, fwd),
            pl.BlockSpec((1, 8, CONV_W), lambda b, i: (b, jnp.maximum(i * r8 - 1, 0), 0)),
            pl.BlockSpec((1, 8, CONV_W), lambda b, i: (b, jnp.minimum((i + 1) * r8, n_i * r8 - 1), 0)),
            const2((1, HEAD_DIM)),
            const2(conv_w.shape),
            const2(w_out.shape),
            const2((1, d)),
            const2(rw_cat.shape),
        ],
        out_specs=(row_block(d, fwd),
                   pl.BlockSpec((MIX_ROWS * ROW_TILES, LANES), lambda b, i: (b * n_i + i, 0)),
                   pl.BlockSpec((N_EXPERTS, MIX_ROWS), lambda b, i: (0, b * n_i + i))),
        scratch_shapes=rec_scratch[:3] + [state],
        compiler_params=cparams,
        name="mixer_fwd",
    )(x, mod, g_mix, lb_logits, w_f, _cumsum_table(False), _level_table(False),
      qv, o_b, uc, uc, uc, norm_g, conv_w, w_out, g_ffn, rw_cat)
    return x1, h2, aff_t


def _threshold_kernel(a_ref, thr_ref, need_ref, *, cap):
    aff = a_ref[...]

    def count(m):
        return jnp.sum(jnp.where(m, 1.0, 0.0), axis=1, keepdims=True)

    def body(it, bits):
        cand = bits | jnp.left_shift(jnp.int32(1), 30 - it)
        return jnp.where(count(aff >= pltpu.bitcast(cand, F32)) >= cap, cand, bits)

    thr = pltpu.bitcast(lax.fori_loop(0, 31, body, jnp.zeros((N_EXPERTS, 1), I32)), F32)
    need = cap - count(aff > thr)
    thr_ref[...] = jnp.broadcast_to(thr, thr_ref.shape)
    need_ref[...] = jnp.broadcast_to(need, need_ref.shape)


def _slot_index_kernel(a_ref, thr_ref, need_ref, idx_ref, pos_ref, offs_ref, *, cap, n_tiles):
    e = pl.program_id(0)
    aff = a_ref[0]
    thr = thr_ref[0]
    need = need_ref[0]
    lane = lax.broadcasted_iota(I32, (n_tiles, LANES), 1)
    upper = jnp.where(lax.broadcasted_iota(I32, (LANES, LANES), 0)
                      <= lax.broadcasted_iota(I32, (LANES, LANES), 1), 1.0, 0.0).astype(BF16)
    below = jnp.where(lax.broadcasted_iota(I32, (n_tiles, n_tiles), 1)
                      < lax.broadcasted_iota(I32, (n_tiles, n_tiles), 0), 1.0, 0.0).astype(BF16)

    def running_count(m):
        within = _dot(jnp.where(m, 1.0, 0.0).astype(BF16), upper)
        total = jnp.broadcast_to(within[:, LANES - 1:LANES], within.shape)
        before = _dot(below, total.astype(BF16))
        return within, before, total

    gt = aff > thr
    eq = aff == thr
    w_eq, b_eq, _ = running_count(eq)
    sel = gt | (eq & ((w_eq + b_eq) <= need))
    within, before, total = running_count(sel)
    cum = within + before
    pos_ref[0] = jnp.where(sel, cum - 1.0 + (e * cap).astype(F32), -1.0).astype(I32)
    offs_ref[0] = before.astype(I32)

    slot = lax.broadcasted_iota(I32, (n_tiles, cap), 1).astype(F32)
    lo = jnp.broadcast_to(before[:, 0:1], (n_tiles, cap))
    hi = jnp.broadcast_to((before + total)[:, 0:1], (n_tiles, cap))
    in_tile = jnp.where((lo <= slot) & (slot < hi), 1.0, 0.0).astype(BF16)
    b_hi = jnp.floor(before * (1.0 / 256.0))
    b_lo = before - 256.0 * b_hi
    tile_id = lax.broadcasted_iota(I32, (n_tiles, LANES), 0).astype(F32)
    extras = jnp.where(lane == 0, b_hi, jnp.where(lane == 1, b_lo, jnp.where(lane == 2, tile_id, 0.0)))
    table = jnp.concatenate([within, extras], axis=1).T.astype(BF16)
    picked = _dot(table, in_tile)
    slot_row = slot[0:1, :]
    before_p = picked[LANES:LANES + 1, :] * 256.0 + picked[LANES + 1:LANES + 2, :]
    tile_p = picked[LANES + 2:LANES + 3, :]
    lane_p = jnp.sum(jnp.where(picked[0:LANES, :] <= slot_row - before_p, 1.0, 0.0),
                     axis=0, keepdims=True)
    idx_ref[0] = (tile_p * float(LANES) + lane_p).astype(I32) * ROW_TILES


def _route(aff_t, cap):
    n = aff_t.shape[1]
    n_tiles = n // LANES
    cparams1 = pltpu.CompilerParams(dimension_semantics=("arbitrary",),
                                    vmem_limit_bytes=VMEM_LIMIT_BYTES)
    thr, need = pl.pallas_call(
        functools.partial(_threshold_kernel, cap=cap),
        out_shape=(jax.ShapeDtypeStruct((N_EXPERTS, LANES), F32),
                   jax.ShapeDtypeStruct((N_EXPERTS, LANES), F32)),
        grid=(1,),
        in_specs=[pl.BlockSpec((N_EXPERTS, n), lambda i: (0, 0))],
        out_specs=(pl.BlockSpec((N_EXPERTS, LANES), lambda i: (0, 0)),
                   pl.BlockSpec((N_EXPERTS, LANES), lambda i: (0, 0))),
        compiler_params=cparams1,
        name="route_threshold",
    )(aff_t)
    per_expert = lambda width: pl.BlockSpec((1, 1, width), lambda e: (e, 0, 0))
    tiles = pl.BlockSpec((1, n_tiles, LANES), lambda e: (e, 0, 0))
    idx, pos, offs = pl.pallas_call(
        functools.partial(_slot_index_kernel, cap=cap, n_tiles=n_tiles),
        out_shape=(jax.ShapeDtypeStruct((N_EXPERTS, 1, cap), I32),
                   jax.ShapeDtypeStruct((N_EXPERTS, n_tiles, LANES), I32),
                   jax.ShapeDtypeStruct((N_EXPERTS, n_tiles, LANES), I32)),
        grid=(N_EXPERTS,),
        in_specs=[tiles, per_expert(LANES), per_expert(LANES)],
        out_specs=(per_expert(cap), tiles, tiles),
        compiler_params=cparams1,
        name="route_slots",
    )(aff_t.reshape(N_EXPERTS, n_tiles, LANES),
      thr.reshape(N_EXPERTS, 1, LANES), need.reshape(N_EXPERTS, 1, LANES))
    return idx, pos, offs


def _ffn_kernel(idx_hbm, h_hbm, wg_ref, wu_ref, wd_ref, y_ref,
                idx_s, buf, sem_i, sem_g, *, n_tiles):
    e = pl.program_id(0)
    pair = e * pl.num_programs(1) + pl.program_id(1)
    last = n_tiles - 1
    tile_rows = FFN_ROWS * ROW_TILES

    def idx_copy(t, sl):
        return pltpu.make_async_copy(idx_hbm.at[jnp.minimum(t, last)], idx_s.at[sl], sem_i.at[sl])

    def row_copy(sl, r, row0):
        return pltpu.make_async_copy(h_hbm.at[pl.ds(row0, ROW_TILES), :],
                                     buf.at[pl.ds((sl * FFN_ROWS + r) * ROW_TILES, ROW_TILES), :],
                                     sem_g.at[sl])

    def all_rows(sl):
        return pltpu.make_async_copy(h_hbm.at[pl.ds(0, tile_rows), :],
                                     buf.at[pl.ds(sl * tile_rows, tile_rows), :], sem_g.at[sl])

    @pl.when(pair == 0)
    def _():
        idx_copy(0, 0).start()
        idx_copy(0, 0).wait()

        def body(r, carry):
            pltpu.make_async_copy(h_hbm.at[pl.ds(idx_s[0, r], ROW_TILES), :],
                                  buf.at[pl.ds(r * ROW_TILES, ROW_TILES), :], sem_g.at[0]).start()
            return carry
        lax.fori_loop(0, FFN_ROWS, body, 0)
        idx_copy(1, 1).start()

    lane = lax.broadcasted_iota(I32, (FFN_ROWS, LANES), 1)

    def tile(t, sl):
        other = 1 - sl
        idx_copy(t + 1, other).wait()
        for r in range(FFN_ROWS):
            row_copy(other, r, idx_s[other, r]).start()
        idx_copy(t + 2, sl).start()
        all_rows(sl).wait()

        parts = [buf[pl.ds(sl * tile_rows + c, FFN_ROWS, stride=ROW_TILES), :] for c in range(ROW_TILES)]
        gate = jnp.sum(jnp.where(lane == e, parts[-1], 0.0), axis=-1, keepdims=True)
        x16 = jnp.concatenate([p.astype(BF16) for p in parts[:-1]], axis=1)
        hid = _silu(_dot(x16, wg_ref[0])) * _dot(x16, wu_ref[0])
        y = _dot(hid.astype(BF16), wd_ref[0]) * gate
        y_ref[sl * FFN_ROWS:(sl + 1) * FFN_ROWS, :] = y.astype(BF16)

    tile(2 * pair, 0)
    tile(2 * pair + 1, 1)

    @pl.when(2 * pair + 1 == last)
    def _():
        idx_copy(last + 2, 1).wait()
        all_rows(0).wait()


def _expert_ffn(idx, h_rows, w_gate, w_up, w_down):
    cap = idx.shape[-1]
    n_j = cap // (2 * FFN_ROWS)
    n_tiles = 2 * N_EXPERTS * n_j
    d, ff = w_gate.shape[1], w_gate.shape[2]
    per_expert = lambda shape: pl.BlockSpec((1,) + shape, lambda e, j: (e, 0, 0))
    return pl.pallas_call(
        functools.partial(_ffn_kernel, n_tiles=n_tiles),
        out_shape=jax.ShapeDtypeStruct((N_EXPERTS * cap, d), BF16),
        grid=(N_EXPERTS, n_j),
        in_specs=[
            pl.BlockSpec(memory_space=pl.ANY),
            pl.BlockSpec(memory_space=pl.ANY),
            per_expert((d, ff)), per_expert((d, ff)), per_expert((ff, d)),
        ],
        out_specs=pl.BlockSpec((2 * FFN_ROWS, d), lambda e, j: (e * n_j + j, 0)),
        scratch_shapes=[
            pltpu.SMEM((2, FFN_ROWS), I32),
            pltpu.VMEM((2 * FFN_ROWS * ROW_TILES, LANES), F32),
            pltpu.SemaphoreType.DMA((2,)),
            pltpu.SemaphoreType.DMA((2,)),
        ],
        compiler_params=pltpu.CompilerParams(
            dimension_semantics=("arbitrary", "arbitrary"), vmem_limit_bytes=VMEM_LIMIT_BYTES),
        name="expert_ffn",
    )(idx.reshape(n_tiles, FFN_ROWS), h_rows, w_gate, w_up, w_down)


def _combine_kernel(offs_ref, x1_ref, pos_ref, mod_ref, g_ref, y_hbm, out_ref,
                    win, acc, sem, *, cap, total_rows):
    i = pl.program_id(0)
    slot = i % 2
    lane = lax.broadcasted_iota(I32, (1, LANES), 1)

    def first_rows(tile):
        return [((e * cap + offs_ref[e, tile]) // BF16_ROWS) * BF16_ROWS for e in range(N_EXPERTS)]

    def window_copies(want_rows, sl):
        copies = []
        for e in range(N_EXPERTS):
            row0 = pl.multiple_of(jnp.minimum(want_rows[e], total_rows - COMB_WIN), BF16_ROWS)
            copies.append(pltpu.make_async_copy(y_hbm.at[pl.ds(row0, COMB_WIN), :],
                                                win.at[sl, pl.ds(e * COMB_WIN, COMB_WIN), :], sem.at[sl]))
        return copies

    def wait_windows(sl):
        for e in range(N_EXPERTS):
            pltpu.make_async_copy(y_hbm.at[pl.ds(0, COMB_WIN), :],
                                  win.at[sl, pl.ds(e * COMB_WIN, COMB_WIN), :], sem.at[sl]).wait()

    def lane_vector(values):
        vec = jnp.zeros((1, LANES), I32)
        for e, val in enumerate(values):
            vec = jnp.where(lane == e, val, vec)
        return vec

    starts = first_rows(i)

    @pl.when(i == 0)
    def _():
        for cp in window_copies(starts, 0):
            cp.start()

    @pl.when(i + 1 < pl.num_programs(0))
    def _():
        for cp in window_copies(first_rows(i + 1), 1 - slot):
            cp.start()

    span = functools.reduce(
        jnp.maximum, [e * cap + offs_ref[e, i + 1] - starts[e] for e in range(N_EXPERTS)])
    n_rounds = (span + COMB_WIN - 1) // COMB_WIN

    expand = jnp.where(
        lax.broadcasted_iota(I32, (LANES, N_EXPERTS * COMB_WIN), 0)
        == lax.broadcasted_iota(I32, (LANES, N_EXPERTS * COMB_WIN), 1) // COMB_WIN, 1.0, 0.0).astype(BF16)
    in_win = (lax.broadcasted_iota(I32, (1, N_EXPERTS * COMB_WIN), 1) % COMB_WIN).astype(F32)
    pos = pos_ref[...]

    def spread(rd):
        want_rows = [s + rd * COMB_WIN for s in starts]
        want = lane_vector(want_rows)
        base = jnp.minimum(want, total_rows - COMB_WIN)
        mine = (pos >= want) & (pos < want + COMB_WIN)
        rel = jnp.where(mine, pos - base, -1).astype(F32).astype(BF16)
        one_hot = jnp.where(_dot(rel, expand) == in_win, 1.0, 0.0).astype(BF16)
        return want_rows, one_hot

    _, one_hot = spread(0)
    wait_windows(slot)
    acc[...] = _dot(one_hot, win[slot])

    def round_body(rd, carry):
        want_rows, one_hot = spread(rd)
        for cp in window_copies(want_rows, slot):
            cp.start()
        wait_windows(slot)
        acc[...] += _dot(one_hot, win[slot])
        return carry

    lax.fori_loop(1, n_rounds, round_body, 0)

    mod = mod_ref[0]
    y = x1_ref[...] + mod[5:6] * acc[...]
    ms = jnp.mean(y * y, axis=-1, keepdims=True)
    out_ref[...] = y * lax.rsqrt(ms + RMS_EPS) * g_ref[...]


def _combine(offs_tab, x1_rows, pos_rows, mod, final_g, y_rows, cap, t_len):
    n, d = x1_rows.shape
    tiles_per_seq = t_len // COMB_ROWS
    total_rows = y_rows.shape[0]
    window = pltpu.VMEM((2, N_EXPERTS * COMB_WIN, d), BF16)
    return pl.pallas_call(
        functools.partial(_combine_kernel, cap=cap, total_rows=total_rows),
        out_shape=jax.ShapeDtypeStruct((n, d), F32),
        grid_spec=pltpu.PrefetchScalarGridSpec(
            num_scalar_prefetch=1,
            grid=(n // COMB_ROWS,),
            in_specs=[
                pl.BlockSpec((COMB_ROWS, d), lambda i, offs: (i, 0)),
                pl.BlockSpec((COMB_ROWS, LANES), lambda i, offs: (i, 0)),
                pl.BlockSpec((1, 6, d), lambda i, offs: (i // tiles_per_seq, 0, 0)),
                pl.BlockSpec((1, d), lambda i, offs: (0, 0)),
                pl.BlockSpec(memory_space=pl.ANY),
            ],
            out_specs=pl.BlockSpec((COMB_ROWS, d), lambda i, offs: (i, 0)),
            scratch_shapes=[window, pltpu.VMEM((COMB_ROWS, d), F32), pltpu.SemaphoreType.DMA((2,))],
        ),
        compiler_params=pltpu.CompilerParams(
            dimension_semantics=("arbitrary",), vmem_limit_bytes=VMEM_LIMIT_BYTES),
        name="moe_combine",
    )(offs_tab, x1_rows, pos_rows, mod, final_g, y_rows)


def _trunk(x, mod, params):
    (g_mix, g_ffn, w_b, w_f, lb_logits, norm_g, conv_w, w_out, rw_cat,
     w_gate, w_up, w_down, final_g) = params
    bsz, t_len, d = x.shape
    n = bsz * t_len
    cap = CAPACITY_FACTOR * n // N_EXPERTS

    x1, h2, aff_t = _mixer(x, mod, g_mix, lb_logits, w_b, w_f, norm_g, conv_w, w_out, g_ffn, rw_cat)
    idx, pos, offs = _route(aff_t, cap)
    y_rows = _expert_ffn(idx, h2, w_gate, w_up, w_down)

    pos_rows = jnp.pad(pos.reshape(N_EXPERTS, n).T, ((0, 0), (0, LANES - N_EXPERTS)), constant_values=-1)
    step = COMB_ROWS // LANES
    offs_tab = jnp.concatenate(
        [offs[:, ::step, 0], jnp.full((N_EXPERTS, 1), cap, I32)], axis=1)
    y = _combine(offs_tab, x1.reshape(n, d), pos_rows, mod, final_g, y_rows, cap, t_len)
    return y.reshape(bsz, t_len, d)


def kernel(x_prompt, x_sample, c_prompt, c_sample, ada_w, ada_b, norm_mix_g, norm_ffn_g, w_in,
           hgrn_lb_logits, hgrn_out_norm_g, conv_w, w_out, router_w, w_gate, w_up, w_down,
           final_norm_g):
    assert ada_w.shape[0] == 1, "one encoder layer"
    d = D_MODEL
    q, zf, zb, v, og, cb, cc, ch = jnp.split(
        w_in[0], [512, 1024, 1536, 2048, 2560, 3072, 3584], axis=1)
    w_b = jnp.concatenate([q, zb, v, cc, ch], axis=1).astype(BF16)
    w_f = jnp.concatenate([zf, og, cb], axis=1).astype(BF16)
    rw = jnp.pad(router_w[0], ((0, 0), (0, LANES - N_EXPERTS)))
    rw_hi = rw.astype(BF16)
    rw_cat = jnp.concatenate([rw_hi, (rw - rw_hi.astype(F32)).astype(BF16)], axis=1)
    params = (norm_mix_g[0].reshape(1, d), norm_ffn_g[0].reshape(1, d), w_b, w_f, hgrn_lb_logits,
              hgrn_out_norm_g[0].reshape(1, HEAD_DIM), conv_w[0], w_out[0].astype(BF16), rw_cat,
              w_gate[0].astype(BF16), w_up[0].astype(BF16), w_down[0].astype(BF16),
              final_norm_g.reshape(1, d))

    nb_p, nb_s = c_prompt.shape[0], c_sample.shape[0]
    c_rows = jnp.concatenate(
        [c_prompt, c_sample, jnp.zeros((8 - nb_p - nb_s, d), F32)], axis=0)
    mod = _modulation(c_rows, ada_w[0], ada_b[0])
    mod_p = mod[:nb_p].reshape(nb_p, 6, d)
    mod_s = mod[nb_p:nb_p + nb_s].reshape(nb_s, 6, d)
    return (_trunk(x_prompt, mod_p, params), _trunk(x_sample, mod_s, params))
```
